```python
import math, functools
import jax, jax.numpy as jnp
from jax import lax
import numpy as np

D_MODEL = 2048
BATCH = 2
SEQ = 4096
DEPTH = 1
DEC_BATCH = 32
DEC_SEQ = 4
PAST_LEN = 16384
PAGE_SIZE = 128

HEAD_DIM = 64
N_HEADS = D_MODEL // 128
ATTN_WIDTH = N_HEADS * HEAD_DIM
SSM_WIDTH = D_MODEL // 2
SSM_GROUP_SIZE = 16
SSM_GROUPS = SSM_WIDTH // SSM_GROUP_SIZE
SSM_STATE = 64
D_FF = 128 * ((8 * D_MODEL // 3 + 127) // 128)
CONV_W = 3
Q_BLOCK = 128
RMS_EPS = 1e-6
DT_MIN = 1e-3
DT_MAX = 1e-1
SB_BIAS_INIT = -6.0
IN_COLS = 3 * ATTN_WIDTH + SSM_WIDTH + 2 * D_MODEL
SPLITS = (ATTN_WIDTH, 2 * ATTN_WIDTH, 3 * ATTN_WIDTH, 3 * ATTN_WIDTH + SSM_WIDTH, 3 * ATTN_WIDTH + SSM_WIDTH + D_MODEL)

kernel_name = "stickbreak_s5_gated_hybrid_step"


def _rmsnorm(x, g):
    xf = x.astype(jnp.float32)
    y = xf * lax.rsqrt(jnp.mean(xf * xf, axis=-1, keepdims=True) + RMS_EPS)
    return (y * g.astype(jnp.float32)).astype(x.dtype)


def _sb_attend(q, q_pos, segments, sb_bias):
    qf = q.astype(jnp.float32) * (HEAD_DIM ** -0.5)
    bias = sb_bias.astype(jnp.float32)[None, :, None, None]
    out = jnp.zeros(q.shape, jnp.float32)
    tail = jnp.zeros((), jnp.float32)
    for k, v, k_pos in reversed(segments):
        z = jnp.einsum('bqhd,bkhd->bhqk', qf, k.astype(jnp.float32)) + bias
        mask = k_pos[None, :] < q_pos[:, None]
        sp = jnp.where(mask, jax.nn.softplus(z), 0.0)
        after = lax.cumsum(sp, axis=3, reverse=True) - sp + tail
        w = jnp.where(mask, jnp.exp(jax.nn.log_sigmoid(z) - after), 0.0)
        out = out + jnp.einsum('bhqk,bkhd->bqhd', w, v.astype(jnp.float32))
        tail = tail + jnp.sum(sp, axis=3, keepdims=True)
    return out.astype(q.dtype)


def _sb_prompt(q, k, v, sb_bias):
    b, L = q.shape[0], q.shape[1]
    nblk = L // Q_BLOCK
    k_pos = jnp.arange(L, dtype=jnp.int32)
    qb = q.reshape(b, nblk, Q_BLOCK, N_HEADS, HEAD_DIM).transpose(1, 0, 2, 3, 4)
    pb = k_pos.reshape(nblk, Q_BLOCK)
    ob = lax.map(lambda a: _sb_attend(a[0], a[1], [(k, v, k_pos)], sb_bias), (qb, pb))
    return ob.transpose(1, 0, 2, 3, 4).reshape(b, L, N_HEADS, HEAD_DIM)


def _sb_sample(q, k, v, sb_bias, k_past, v_past):
    n_past = k_past.shape[1]
    past_pos = jnp.arange(n_past, dtype=jnp.int32)
    new_pos = n_past + jnp.arange(q.shape[1], dtype=jnp.int32)
    return _sb_attend(q, new_pos, [(k_past, v_past, past_pos), (k, v, new_pos)], sb_bias)


def _ssm_combine(e1, e2):
    a1r, a1i, b1r, b1i = e1
    a2r, a2i, b2r, b2i = e2
    return (a2r * a1r - a2i * a1i,
            a2r * a1i + a2i * a1r,
            a2r * b1r - a2i * b1i + b2r,
            a2r * b1i + a2i * b1r + b2i)


def _ssm_branch(u, h0_re, h0_im, a_re, a_im, log_dt, b_re, b_im, c_re, c_im, d_skip, w_glu, b_glu):
    bsz, L, _ = u.shape
    f32 = jnp.float32
    ar, ai = a_re.astype(f32), a_im.astype(f32)
    dt = jnp.exp(log_dt.astype(f32))[:, None]
    mag = jnp.exp(dt * ar)
    abar_re, abar_im = mag * jnp.cos(dt * ai), mag * jnp.sin(dt * ai)
    nr, ni = abar_re - 1.0, abar_im
    den = ar * ar + ai * ai
    fr, fi = (nr * ar + ni * ai) / den, (ni * ar - nr * ai) / den
    br, bi = b_re.astype(f32), b_im.astype(f32)
    bbar_re = fr[..., None] * br - fi[..., None] * bi
    bbar_im = fr[..., None] * bi + fi[..., None] * br
    ug = u.astype(f32).reshape(bsz, L, SSM_GROUPS, SSM_GROUP_SIZE)
    bu_re = jnp.einsum('gpc,blgc->blgp', bbar_re, ug)
    bu_im = jnp.einsum('gpc,blgc->blgp', bbar_im, ug)
    a_b_re = jnp.broadcast_to(abar_re, bu_re.shape)
    a_b_im = jnp.broadcast_to(abar_im, bu_im.shape)
    acr, aci, hr, hi = lax.associative_scan(_ssm_combine, (a_b_re, a_b_im, bu_re, bu_im), axis=1)
    h0r, h0i = h0_re.astype(f32)[:, None], h0_im.astype(f32)[:, None]
    hr = hr + acr * h0r - aci * h0i
    hi = hi + acr * h0i + aci * h0r
    y = jnp.einsum('gcp,blgp->blgc', c_re.astype(f32), hr) - jnp.einsum('gcp,blgp->blgc', c_im.astype(f32), hi)
    y = y.reshape(bsz, L, SSM_WIDTH) + d_skip.astype(f32) * u.astype(f32)
    g = jax.nn.gelu(y)
    y = g * jax.nn.sigmoid(g @ w_glu.astype(f32) + b_glu.astype(f32))
    return y.astype(u.dtype), hr[:, -1], hi[:, -1]


def _causal_dwconv(u, prev, w, bias):
    L = u.shape[1]
    ext = jnp.concatenate([prev.astype(u.dtype), u], axis=1)
    y = bias + w[0] * ext[:, 0:L]
    for i in range(1, CONV_W):
        y = y + w[i] * ext[:, i:i + L]
    return y, ext[:, -(CONV_W - 1):]


def _layer(x, attn_fn, h0_re, h0_im, conv_prev, norm_mix_pre, norm_mix_post, norm_ffn_pre, norm_ffn_post,
           w_in, sb_bias, ssm_a_re, ssm_a_im, ssm_log_dt, ssm_b_re, ssm_b_im, ssm_c_re, ssm_c_im, ssm_d,
           ssm_w_glu, ssm_b_glu, w_branch_attn, w_branch_ssm, w_out, w_up, conv_w, conv_b, w_down):
    bsz, L, _ = x.shape
    xn = _rmsnorm(x, norm_mix_pre)
    proj = xn @ w_in
    q, k, v, u, g_attn, g_ssm = jnp.split(proj, SPLITS, axis=-1)
    q = q.reshape(bsz, L, N_HEADS, HEAD_DIM)
    k = k.reshape(bsz, L, N_HEADS, HEAD_DIM)
    v = v.reshape(bsz, L, N_HEADS, HEAD_DIM)
    o_attn = attn_fn(q, k, v, sb_bias).reshape(bsz, L, ATTN_WIDTH)
    o_ssm, h_re, h_im = _ssm_branch(u, h0_re, h0_im, ssm_a_re, ssm_a_im, ssm_log_dt, ssm_b_re, ssm_b_im,
                                    ssm_c_re, ssm_c_im, ssm_d, ssm_w_glu, ssm_b_glu)
    merged = jax.nn.sigmoid(g_attn) * (o_attn @ w_branch_attn) + jax.nn.sigmoid(g_ssm) * (o_ssm @ w_branch_ssm)
    x = x + _rmsnorm(merged @ w_out, norm_mix_post)
    xn = _rmsnorm(x, norm_ffn_pre)
    up = xn @ w_up
    gate, val = up[..., :D_FF], up[..., D_FF:]
    gate, conv_state = _causal_dwconv(gate, conv_prev, conv_w, conv_b)
    h = jax.nn.gelu(gate) * val
    x = x + _rmsnorm(h @ w_down, norm_ffn_post)
    return x, k, v, h_re, h_im, conv_state


def setup_inputs(seed: int = 0) -> dict:
    key = jax.random.key(seed)
    ks = jax.random.split(key, 40)
    f32 = jnp.float32
    n_pages = PAST_LEN // PAGE_SIZE
    n_used = DEC_BATCH * n_pages
    n_pool = n_used + max(1, n_used // 4)
    nrm = lambda i, shape, s: s * jax.random.normal(ks[i], shape, f32)
    page_table = jax.random.permutation(ks[0], n_pool)[:n_used].reshape(DEC_BATCH, n_pages).astype(jnp.int32)
    a_im0 = math.pi * jnp.arange(SSM_STATE, dtype=f32)
    return {
        "x_prompt": nrm(1, (BATCH, SEQ, D_MODEL), 1.0),
        "x_sample": nrm(2, (DEC_BATCH, DEC_SEQ, D_MODEL), 1.0),
        "cache_k": nrm(3, (DEPTH, n_pool, PAGE_SIZE, N_HEADS, HEAD_DIM), 1.0),
        "cache_v": nrm(4, (DEPTH, n_pool, PAGE_SIZE, N_HEADS, HEAD_DIM), 1.0),
        "page_table": page_table,
        "state_ssm_re": nrm(5, (DEPTH, DEC_BATCH, SSM_GROUPS, SSM_STATE), 0.3),
        "state_ssm_im": nrm(6, (DEPTH, DEC_BATCH, SSM_GROUPS, SSM_STATE), 0.3),
        "state_ffn_conv": nrm(7, (DEPTH, DEC_BATCH, CONV_W - 1, D_FF), 1.0),
        "norm_mix_pre": 1.0 + nrm(8, (DEPTH, D_MODEL), 0.05),
        "norm_mix_post": 1.0 + nrm(9, (DEPTH, D_MODEL), 0.05),
        "norm_ffn_pre": 1.0 + nrm(10, (DEPTH, D_MODEL), 0.05),
        "norm_ffn_post": 1.0 + nrm(11, (DEPTH, D_MODEL), 0.05),
        "w_in": nrm(12, (DEPTH, D_MODEL, IN_COLS), D_MODEL ** -0.5),
        "sb_bias": SB_BIAS_INIT + nrm(30, (DEPTH, N_HEADS), 0.1),
        "ssm_a_re": -0.5 + nrm(13, (DEPTH, SSM_GROUPS, SSM_STATE), 0.01),
        "ssm_a_im": a_im0 + nrm(14, (DEPTH, SSM_GROUPS, SSM_STATE), 0.01),
        "ssm_log_dt": jax.random.uniform(ks[15], (DEPTH, SSM_GROUPS), f32, math.log(DT_MIN), math.log(DT_MAX)),
        "ssm_b_re": nrm(16, (DEPTH, SSM_GROUPS, SSM_STATE, SSM_GROUP_SIZE), SSM_GROUP_SIZE ** -0.5),
        "ssm_b_im": nrm(17, (DEPTH, SSM_GROUPS, SSM_STATE, SSM_GROUP_SIZE), SSM_GROUP_SIZE ** -0.5),
        "ssm_c_re": nrm(18, (DEPTH, SSM_GROUPS, SSM_GROUP_SIZE, SSM_STATE), SSM_STATE ** -0.5),
        "ssm_c_im": nrm(19, (DEPTH, SSM_GROUPS, SSM_GROUP_SIZE, SSM_STATE), SSM_STATE ** -0.5),
        "ssm_d": nrm(20, (DEPTH, SSM_WIDTH), 1.0),
        "ssm_w_glu": nrm(21, (DEPTH, SSM_WIDTH, SSM_WIDTH), SSM_WIDTH ** -0.5),
        "ssm_b_glu": nrm(22, (DEPTH, SSM_WIDTH), 0.01),
        "w_branch_attn": nrm(23, (DEPTH, ATTN_WIDTH, D_MODEL), ATTN_WIDTH ** -0.5),
        "w_branch_ssm": nrm(24, (DEPTH, SSM_WIDTH, D_MODEL), SSM_WIDTH ** -0.5),
        "w_out": nrm(25, (DEPTH, D_MODEL, D_MODEL), D_MODEL ** -0.5),
        "w_up": nrm(26, (DEPTH, D_MODEL, 2 * D_FF), D_MODEL ** -0.5),
        "conv_w": nrm(27, (DEPTH, CONV_W, D_FF), 0.5),
        "conv_b": nrm(28, (DEPTH, D_FF), 0.01),
        "w_down": nrm(29, (DEPTH, D_FF, D_MODEL), D_FF ** -0.5),
    }


def reference(x_prompt, x_sample, cache_k, cache_v, page_table, state_ssm_re, state_ssm_im, state_ffn_conv,
              norm_mix_pre, norm_mix_post, norm_ffn_pre, norm_ffn_post, w_in, sb_bias, ssm_a_re, ssm_a_im,
              ssm_log_dt, ssm_b_re, ssm_b_im, ssm_c_re, ssm_c_im, ssm_d, ssm_w_glu, ssm_b_glu, w_branch_attn,
              w_branch_ssm, w_out, w_up, conv_w, conv_b, w_down):
    bp, bs = x_prompt.shape[0], x_sample.shape[0]
    xp, xs = x_prompt, x_sample
    kp_l, vp_l, ks_l, vs_l = [], [], [], []
    hrp_l, hip_l, hrs_l, his_l, cp_l, cs_l = [], [], [], [], [], []
    for layer in range(DEPTH):
        wl = (norm_mix_pre[layer], norm_mix_post[layer], norm_ffn_pre[layer], norm_ffn_post[layer], w_in[layer],
              sb_bias[layer], ssm_a_re[layer], ssm_a_im[layer], ssm_log_dt[layer], ssm_b_re[layer],
              ssm_b_im[layer], ssm_c_re[layer], ssm_c_im[layer], ssm_d[layer], ssm_w_glu[layer],
              ssm_b_glu[layer], w_branch_attn[layer], w_branch_ssm[layer], w_out[layer], w_up[layer],
              conv_w[layer], conv_b[layer], w_down[layer])
        h0 = jnp.zeros((bp, SSM_GROUPS, SSM_STATE), jnp.float32)
        c0 = jnp.zeros((bp, CONV_W - 1, D_FF), xp.dtype)
        xp, kp, vp, hrp, hip, cp = _layer(xp, _sb_prompt, h0, h0, c0, *wl)
        k_past = cache_k[layer][page_table].reshape(bs, -1, N_HEADS, HEAD_DIM)
        v_past = cache_v[layer][page_table].reshape(bs, -1, N_HEADS, HEAD_DIM)
        attn_s = functools.partial(_sb_sample, k_past=k_past, v_past=v_past)
        xs, ksn, vsn, hrs, his, cs = _layer(xs, attn_s, state_ssm_re[layer], state_ssm_im[layer],
                                            state_ffn_conv[layer], *wl)
        kp_l.append(kp); vp_l.append(vp); ks_l.append(ksn); vs_l.append(vsn)
        hrp_l.append(hrp); hip_l.append(hip); hrs_l.append(hrs); his_l.append(his)
        cp_l.append(cp); cs_l.append(cs)
    k_prompt, v_prompt = jnp.stack(kp_l), jnp.stack(vp_l)
    k_sample, v_sample = jnp.stack(ks_l), jnp.stack(vs_l)
    ssm_re_prompt, ssm_im_prompt = jnp.stack(hrp_l), jnp.stack(hip_l)
    ssm_re_sample, ssm_im_sample = jnp.stack(hrs_l), jnp.stack(his_l)
    conv_prompt, conv_sample = jnp.stack(cp_l), jnp.stack(cs_l)
    return (xp, xs, k_prompt, v_prompt, k_sample, v_sample, ssm_re_prompt, ssm_im_prompt,
            ssm_re_sample, ssm_im_sample, conv_prompt, conv_sample)
```

```python
import functools
import math

import jax
import jax.numpy as jnp
from jax import lax
from jax.experimental import pallas as pl
from jax.experimental.pallas import tpu as pltpu

RMS_EPS = 1e-6
HEAD_DIM = 64
HEADS_PER_BLOCK = 2
LANES = 128
SUBLANES = 8
SSM_BLOCK_CH = 128
VMEM_LIMIT = 56 * 1024 * 1024
ATTN_TILE = 256
FFN_COL_TILE = 512
BF16 = jnp.bfloat16
F32 = jnp.float32


def _cparams(sem):
    return pltpu.CompilerParams(dimension_semantics=sem, vmem_limit_bytes=VMEM_LIMIT)


def _pick_tile(n, target, quantum=SUBLANES):
    best = None
    for t in range(quantum, min(n, target) + 1, quantum):
        if n % t == 0:
            best = t
    assert best is not None, (n, target, quantum)
    return best


def _dot(a, b):
    return jnp.dot(a, b, preferred_element_type=F32)


def _dot_nt(a, b):
    return lax.dot_general(a, b, (((1,), (1,)), ((), ())), preferred_element_type=F32)


def _sigmoid(x):
    return 1.0 / (1.0 + jnp.exp(-x))


def _gelu(x):
    c = math.sqrt(2.0 / math.pi)
    return 0.5 * x * (1.0 + jnp.tanh(c * (x + 0.044715 * (x * x * x))))


def _rms(x):
    return x * lax.rsqrt(jnp.mean(x * x, axis=-1, keepdims=True) + RMS_EPS)


def _softplus(z):
    return jnp.maximum(z, 0.0) + jnp.log(1.0 + jnp.exp(-jnp.abs(z)))


def _ssm_discretise(ar, ai, ldt):
    dt = jnp.exp(ldt)
    mag = jnp.exp(dt * ar)
    return mag * jnp.cos(dt * ai), mag * jnp.sin(dt * ai)


def _ssm_prep_kernel(ar_ref, ai_ref, ldt_ref, ar3_ref, ai3_ref, ldt3_ref, br_ref, bi_ref,
                     powr_ref, powi_ref, bbr_ref, bbi_ref):
    ar, ai = ar3_ref[...], ai3_ref[...]
    abr, abi = _ssm_discretise(ar, ai, ldt3_ref[...])
    nr, ni = abr - 1.0, abi
    den = ar * ar + ai * ai
    fr, fi = (nr * ar + ni * ai) / den, (ni * ar - nr * ai) / den
    br, bi = br_ref[...], bi_ref[...]
    bbr_ref[...] = fr * br - fi * bi
    bbi_ref[...] = fr * bi + fi * br
    abr, abi = _ssm_discretise(ar_ref[...], ai_ref[...], ldt_ref[...])
    pr, pi = abr, abi
    powr_ref[0], powi_ref[0] = pr, pi
    for k in range(1, SUBLANES):
        pr, pi = pr * abr - pi * abi, pr * abi + pi * abr
        powr_ref[k], powi_ref[k] = pr, pi


def _ssm_prep(a_re, a_im, log_dt, b_re, b_im):
    g, p = a_re.shape
    c = b_re.shape[-1]
    out = (jax.ShapeDtypeStruct((SUBLANES, g, p), F32),) * 2 + (jax.ShapeDtypeStruct((g, c, p), F32),) * 2
    return pl.pallas_call(_ssm_prep_kernel, out_shape=out, name="ssm_prep")(
        a_re, a_im, log_dt.reshape(g, 1), a_re.reshape(g, 1, p), a_im.reshape(g, 1, p), log_dt.reshape(g, 1, 1),
        b_re.transpose(0, 2, 1), b_im.transpose(0, 2, 1))


def _ssm_block_weights(bbr, bbi, c_re, c_im):
    g, c, p = bbr.shape
    gb = SSM_BLOCK_CH // c
    nb = g // gb
    eye = jnp.eye(gb, dtype=F32)
    bb = jnp.stack([bbr, bbi], axis=0).reshape(2, nb, gb, c, p)
    bblk = jnp.einsum("rjgcp,gh->jgcrhp", bb, eye).reshape(nb, gb * c, 2 * gb * p)
    cc = jnp.stack([c_re, -c_im], axis=0).reshape(2, nb, gb, c, p)
    cblk = jnp.einsum("rjgcp,gh->jrgphc", cc, eye).reshape(nb, 2 * gb * p, gb * c)
    return bblk.astype(BF16), cblk.astype(BF16)


def _scan_consts(powr, powi, seg, gb):
    _, g, p = powr.shape
    nb = g // gb
    rr = jnp.arange(SUBLANES) % seg
    kinds = []
    for d in (1, 2, 4):
        m = (rr >= d).astype(F32)[:, None, None]
        kinds += [m * powr[d - 1][None], m * powi[d - 1][None]]
    kinds += [powr[rr], powi[rr]]
    k = jnp.stack(kinds, axis=0)
    return k.reshape(8, SUBLANES, nb, gb * p).transpose(2, 0, 1, 3)


def _inproj_kernel(x_ref, g_ref, w_ref, *rest, ranges):
    out_refs, xn_ref = rest[:-1], rest[-1]
    j = pl.program_id(1)

    @pl.when(j == 0)
    def _():
        xn_ref[...] = (_rms(x_ref[...]) * g_ref[...]).astype(BF16)

    acc = _dot(xn_ref[...], w_ref[...])
    for ref, (lo, hi) in zip(out_refs, ranges):
        @pl.when((j >= lo) & (j < hi))
        def _(ref=ref):
            ref[...] = acc


def _in_proj(x, g, w_bf16, widths, tm):
    r, d = x.shape
    n = w_bf16.shape[1]
    tn = 512
    ranges, specs, shapes, lo = [], [], [], 0
    for wd in widths:
        nblk = wd // tn
        ranges.append((lo, lo + nblk))
        specs.append(pl.BlockSpec((tm, tn), lambda i, j, lo=lo, nblk=nblk: (i, jnp.clip(j - lo, 0, nblk - 1))))
        shapes.append(jax.ShapeDtypeStruct((r, wd), F32))
        lo += nblk
    assert lo * tn == n
    return pl.pallas_call(
        functools.partial(_inproj_kernel, ranges=tuple(ranges)),
        grid=(r // tm, n // tn),
        in_specs=[pl.BlockSpec((tm, d), lambda i, j: (i, 0)),
                  pl.BlockSpec((1, d), lambda i, j: (0, 0)),
                  pl.BlockSpec((d, tn), lambda i, j: (0, j))],
        out_specs=specs, out_shape=shapes,
        scratch_shapes=[pltpu.VMEM((tm, d), BF16)],
        compiler_params=_cparams(("arbitrary", "arbitrary")), name="in_proj")(x, g.reshape(1, d), w_bf16)


def _sb_block(z, mask, tail, tri, v_bf16):
    sp = _softplus(z)
    if mask is not None:
        sp = jnp.where(mask, sp, 0.0)
    hi = sp.astype(BF16)
    lo = (sp - hi.astype(F32)).astype(BF16)
    rsum = _dot(hi, tri) + _dot(lo, tri) + tail
    w = jnp.exp(z - rsum)
    if mask is not None:
        w = jnp.where(mask, w, 0.0)
    return _dot(w.astype(BF16), v_bf16), rsum[:, 0:1]


def _attn_prompt_kernel(bias_ref, q_ref, k_ref, v_ref, tri_ref, o_ref, kb_ref, vb_ref, acc_ref, *, t):
    hp, qi = pl.program_id(1), pl.program_id(2)

    @pl.when(qi == 0)
    def _():
        kb_ref[...] = k_ref[...].astype(BF16)
        vb_ref[...] = v_ref[...].astype(BF16)

    lane = lax.broadcasted_iota(jnp.int32, (1, LANES), 1)
    first = lane < HEAD_DIM
    q = q_ref[...] * (HEAD_DIM ** -0.5)
    qh = (jnp.where(first, q, 0.0).astype(BF16), jnp.where(first, 0.0, q).astype(BF16))
    bias = (bias_ref[HEADS_PER_BLOCK * hp], bias_ref[HEADS_PER_BLOCK * hp + 1])
    tri = tri_ref[...]
    row = lax.broadcasted_iota(jnp.int32, (t, t), 0)
    col = lax.broadcasted_iota(jnp.int32, (t, t), 1)
    diag_mask = col < row

    def block(j, h, tail, mask):
        start = pl.multiple_of(j * t, t)
        z = _dot_nt(qh[h], kb_ref[pl.ds(start, t), :]) + bias[h]
        return _sb_block(z, mask, tail, tri, vb_ref[pl.ds(start, t), :])

    zero_tail = jnp.zeros((t, 1), F32)
    tails = []
    for h in range(HEADS_PER_BLOCK):
        pv, tl = block(qi, h, zero_tail, diag_mask)
        acc_ref[h] = pv
        tails.append(tl)

    def body(it, tails):
        j = qi - 1 - it
        new = []
        for h in range(HEADS_PER_BLOCK):
            pv, tl = block(j, h, tails[h], None)
            acc_ref[h] += pv
            new.append(tl)
        return tuple(new)

    lax.fori_loop(0, qi, body, tuple(tails))
    o_ref[...] = jnp.where(first, acc_ref[0], acc_ref[1])


def _attn_prompt(q, k, v, sb_bias, bsz, seq):
    r, width = q.shape
    t = _pick_tile(seq, ATTN_TILE, LANES)
    nq = seq // t
    nhp = width // LANES
    tri = (jnp.arange(t)[:, None] >= jnp.arange(t)[None, :]).astype(BF16)
    return pl.pallas_call(
        functools.partial(_attn_prompt_kernel, t=t),
        grid=(bsz, nhp, nq),
        in_specs=[pl.BlockSpec(memory_space=pltpu.SMEM),
                  pl.BlockSpec((t, LANES), lambda b, hp, qi: (b * nq + qi, hp)),
                  pl.BlockSpec((seq, LANES), lambda b, hp, qi: (b, hp)),
                  pl.BlockSpec((seq, LANES), lambda b, hp, qi: (b, hp)),
                  pl.BlockSpec((t, t), lambda b, hp, qi: (0, 0))],
        out_specs=pl.BlockSpec((t, LANES), lambda b, hp, qi: (b * nq + qi, hp)),
        out_shape=jax.ShapeDtypeStruct((r, width), F32),
        scratch_shapes=[pltpu.VMEM((seq, LANES), BF16), pltpu.VMEM((seq, LANES), BF16),
                        pltpu.VMEM((HEADS_PER_BLOCK, t, LANES), F32)],
        compiler_params=_cparams(("arbitrary", "arbitrary", "arbitrary")), name="attn_prompt")(
            sb_bias, q, k, v, tri)


def _attn_sample_kernel(pt_ref, q_ref, kn_ref, vn_ref, bias_ref, tri_ref, *rest, n_heads, n_q, pages_per_step):
    del pt_ref
    k_refs = rest[:pages_per_step]
    v_refs = rest[pages_per_step:2 * pages_per_step]
    o_ref, wq_ref, acc_ref, tail_ref, pad_ref = rest[2 * pages_per_step:]
    s = pl.program_id(1)
    nrow = n_q * n_heads
    width = n_heads * HEAD_DIM
    assert n_heads & (n_heads - 1) == 0 and HEAD_DIM & (HEAD_DIM - 1) == 0
    head_shift, dim_shift = n_heads.bit_length() - 1, HEAD_DIM.bit_length() - 1
    r_head = lax.broadcasted_iota(jnp.int32, (nrow, width), 0) & (n_heads - 1)
    c_head = lax.broadcasted_iota(jnp.int32, (nrow, width), 1) >> dim_shift
    own_head = r_head == c_head
    page = tri_ref.shape[0]
    tri = tri_ref[...]
    bias = bias_ref[...]

    @pl.when(s == 0)
    def _():
        q = q_ref[0] * (HEAD_DIM ** -0.5)
        rows = jnp.concatenate([jnp.broadcast_to(q[i:i + 1], (n_heads, width)) for i in range(n_q)], axis=0)
        wq_ref[...] = jnp.where(own_head, rows, 0.0).astype(BF16)
        pad_ref[...] = jnp.zeros_like(pad_ref)
        pad_ref[0:SUBLANES, :] = kn_ref[0]
        z = _dot_nt(wq_ref[...], pad_ref[...].astype(BF16)) + bias
        pad_ref[0:SUBLANES, :] = vn_ref[0]
        kpos = lax.broadcasted_iota(jnp.int32, (nrow, page), 1)
        qpos = lax.broadcasted_iota(jnp.int32, (nrow, page), 0) >> head_shift
        pv, tl = _sb_block(z, kpos < qpos, jnp.zeros((nrow, 1), F32), tri, pad_ref[...].astype(BF16))
        acc_ref[...] = pv
        tail_ref[...] = tl

    tail = tail_ref[...]
    acc = acc_ref[...]
    for p in range(pages_per_step):
        z = _dot_nt(wq_ref[...], k_refs[p][0].astype(BF16)) + bias
        pv, tail = _sb_block(z, None, tail, tri, v_refs[p][0].astype(BF16))
        acc = acc + pv
    acc_ref[...] = acc
    tail_ref[...] = tail

    @pl.when(s == pl.num_programs(1) - 1)
    def _():
        own = jnp.where(own_head, acc, 0.0)
        o_ref[0] = jnp.concatenate(
            [jnp.sum(own[i * n_heads:(i + 1) * n_heads], axis=0, keepdims=True) for i in range(n_q)], axis=0)


def _attn_sample(q, k_new, v_new, sb_bias, cache_k, cache_v, page_table, bs, n_q):
    n_pool, page, n_heads, hd = cache_k.shape
    width = n_heads * hd
    n_pages = page_table.shape[1]
    pps = _pick_tile(n_pages, 8, 1)
    nrow = n_q * n_heads
    ck = cache_k.reshape(n_pool, page, width)
    cv = cache_v.reshape(n_pool, page, width)
    pad = lambda a: jnp.pad(a.reshape(bs, n_q, width), ((0, 0), (0, SUBLANES - n_q), (0, 0)))
    bias_col = jnp.tile(sb_bias, n_q).reshape(nrow, 1)
    tri = (jnp.arange(page)[:, None] >= jnp.arange(page)[None, :]).astype(BF16)

    def page_spec(p):
        return pl.BlockSpec((1, page, width),
                            lambda b, s, pt, p=p: (pt[b, n_pages - 1 - (s * pps + p)], 0, 0))

    grid_spec = pltpu.PrefetchScalarGridSpec(
        num_scalar_prefetch=1, grid=(bs, n_pages // pps),
        in_specs=[pl.BlockSpec((1, n_q, width), lambda b, s, pt: (b, 0, 0)),
                  pl.BlockSpec((1, SUBLANES, width), lambda b, s, pt: (b, 0, 0)),
                  pl.BlockSpec((1, SUBLANES, width), lambda b, s, pt: (b, 0, 0)),
                  pl.BlockSpec((nrow, 1), lambda b, s, pt: (0, 0)),
                  pl.BlockSpec((page, page), lambda b, s, pt: (0, 0))]
                 + [page_spec(p) for p in range(pps)] * 2,
        out_specs=pl.BlockSpec((1, n_q, width), lambda b, s, pt: (b, 0, 0)),
        scratch_shapes=[pltpu.VMEM((nrow, width), BF16), pltpu.VMEM((nrow, width), F32),
                        pltpu.VMEM((nrow, 1), F32), pltpu.VMEM((page, width), F32)])
    out = pl.pallas_call(
        functools.partial(_attn_sample_kernel, n_heads=n_heads, n_q=n_q, pages_per_step=pps),
        grid_spec=grid_spec, out_shape=jax.ShapeDtypeStruct((bs, n_q, width), F32),
        compiler_params=_cparams(("arbitrary", "arbitrary")), name="attn_sample")(
            page_table, q.reshape(bs, n_q, width), pad(k_new), pad(v_new), bias_col, tri,
            *([ck] * pps), *([cv] * pps))
    return out.reshape(bs * n_q, width)


def _ssm_kernel(*refs, chunks_per_seq, per_row_h0):
    if per_row_h0:
        u_ref, bw_ref, cw_ref, k_ref, d_ref, h0_ref, gg_ref, hs_ref, bu_ref = refs
    else:
        u_ref, bw_ref, cw_ref, k_ref, d_ref, gg_ref, hs_ref, bu_ref, carry_ref = refs
    i, j = pl.program_id(0), pl.program_id(1)
    rows = u_ref.shape[0]
    half = bu_ref.shape[1] // 2
    u = u_ref[...]
    bu_ref[...] = _dot(u.astype(BF16), bw_ref[0])

    if not per_row_h0:
        @pl.when(i % chunks_per_seq == 0)
        def _():
            carry_ref[j] = jnp.zeros((1, 2 * half), F32)

    def tile(tix, carry):
        r0 = pl.multiple_of(tix * SUBLANES, SUBLANES)
        br = bu_ref[pl.ds(r0, SUBLANES), 0:half]
        bi = bu_ref[pl.ds(r0, SUBLANES), half:2 * half]
        for n, d in enumerate((1, 2, 4)):
            ar, ai = k_ref[0, 2 * n], k_ref[0, 2 * n + 1]
            sr, si = pltpu.roll(br, d, 0), pltpu.roll(bi, d, 0)
            br, bi = br + ar * sr - ai * si, bi + ar * si + ai * sr
        pr, pi = k_ref[0, 6], k_ref[0, 7]
        if per_row_h0:
            cr = h0_ref[pl.ds(r0, SUBLANES), 0:half]
            ci = h0_ref[pl.ds(r0, SUBLANES), half:2 * half]
        else:
            cr, ci = carry
        hr = br + pr * cr - pi * ci
        hi = bi + pr * ci + pi * cr
        bu_ref[pl.ds(r0, SUBLANES), 0:half] = hr
        bu_ref[pl.ds(r0, SUBLANES), half:2 * half] = hi
        return (hr[SUBLANES - 1:SUBLANES], hi[SUBLANES - 1:SUBLANES])

    if per_row_h0:
        init = (jnp.zeros((1, half), F32),) * 2
    else:
        init = (carry_ref[j, :, 0:half], carry_ref[j, :, half:2 * half])
    last = lax.fori_loop(0, rows // SUBLANES, tile, init)
    if per_row_h0:
        hs_ref[...] = bu_ref[...]
    else:
        carry_ref[j, :, 0:half] = last[0]
        carry_ref[j, :, half:2 * half] = last[1]
        hs_ref[0, 0] = bu_ref[rows - SUBLANES:rows, :]
    y = _dot(bu_ref[...].astype(BF16), cw_ref[0]) + d_ref[...] * u
    gg_ref[...] = _gelu(y)


def _ssm(u, bblk, cblk, consts, d_skip, seq, tc, h0_rows=None):
    r, width = u.shape
    nb, cb, sw = bblk.shape
    per_row = h0_rows is not None
    cps = max(seq // tc, 1)
    in_specs = [pl.BlockSpec((tc, cb), lambda i, j: (i, j)),
                pl.BlockSpec((1, cb, sw), lambda i, j: (j, 0, 0)),
                pl.BlockSpec((1, sw, cb), lambda i, j: (j, 0, 0)),
                pl.BlockSpec((1, 8, SUBLANES, sw // 2), lambda i, j: (j, 0, 0, 0)),
                pl.BlockSpec((1, cb), lambda i, j: (0, j))]
    args = [u, bblk, cblk, consts, d_skip.reshape(1, width)]
    scratch = [pltpu.VMEM((tc, sw), F32)]
    if per_row:
        assert r == tc
        in_specs.append(pl.BlockSpec((tc, sw), lambda i, j: (0, j)))
        args.append(h0_rows)
        hs_shape = jax.ShapeDtypeStruct((r, nb * sw), F32)
        hs_spec = pl.BlockSpec((tc, sw), lambda i, j: (0, j))
    else:
        hs_shape = jax.ShapeDtypeStruct((r // seq, nb, SUBLANES, sw), F32)
        hs_spec = pl.BlockSpec((1, 1, SUBLANES, sw), lambda i, j: (i // cps, j, 0, 0))
        scratch.append(pltpu.VMEM((nb, 1, sw), F32))
    return pl.pallas_call(
        functools.partial(_ssm_kernel, chunks_per_seq=cps, per_row_h0=per_row),
        grid=(r // tc, nb), in_specs=in_specs,
        out_specs=[pl.BlockSpec((tc, cb), lambda i, j: (i, j)), hs_spec],
        out_shape=[jax.ShapeDtypeStruct((r, width), F32), hs_shape],
        scratch_shapes=scratch,
        compiler_params=_cparams(("arbitrary", "arbitrary")), name="ssm")(*args)


def _mix_kernel(oa_ref, gg_ref, ga_ref, gs_ref, x_ref, wglu_ref, bglu_ref, wba_ref, wbs_ref, wout_ref, g_ref, x1_ref):
    gg = gg_ref[...]
    o_ssm = gg * _sigmoid(_dot(gg.astype(BF16), wglu_ref[...]) + bglu_ref[...])
    merged = (_sigmoid(ga_ref[...]) * _dot(oa_ref[...].astype(BF16), wba_ref[...])
              + _sigmoid(gs_ref[...]) * _dot(o_ssm.astype(BF16), wbs_ref[...]))
    mo = _dot(merged.astype(BF16), wout_ref[...])
    x1_ref[...] = x_ref[...] + _rms(mo) * g_ref[...]


def _mix(o_attn, gg, ga, gs, x, wglu, bglu, wba, wbs, wout, g, tm):
    r, d = x.shape
    w = o_attn.shape[1]
    row = lambda n: pl.BlockSpec((tm, n), lambda i: (i, 0))
    const = lambda a, b: pl.BlockSpec((a, b), lambda i: (0, 0), pipeline_mode=pl.Buffered(1))
    return pl.pallas_call(
        _mix_kernel, grid=(r // tm,),
        in_specs=[row(w), row(w), row(d), row(d), row(d), const(w, w), const(1, w), const(w, d), const(w, d),
                  const(d, d), const(1, d)],
        out_specs=row(d), out_shape=jax.ShapeDtypeStruct((r, d), F32),
        compiler_params=_cparams(("arbitrary",)), name="mix")(
            o_attn, gg, ga, gs, x, wglu, bglu.reshape(1, w), wba, wbs, wout, g.reshape(1, d))


def _ffn_kernel(*refs, seq, has_prev, conv_w):
    if has_prev:
        (x_ref, g3_ref, wg_ref, wv_ref, cw_ref, cb_ref, wd_ref, g4_ref, p1_ref, p2_ref,
         y_ref, cs_ref, xn_ref, acc_ref, gbuf_ref, carry_ref) = refs
    else:
        (x_ref, g3_ref, wg_ref, wv_ref, cw_ref, cb_ref, wd_ref, g4_ref,
         y_ref, cs_ref, xn_ref, acc_ref, gbuf_ref, carry_ref) = refs
    i, j = pl.program_id(0), pl.program_id(1)
    tm = x_ref.shape[0]
    assert conv_w == 3

    @pl.when(j == 0)
    def _():
        xn_ref[...] = (_rms(x_ref[...]) * g3_ref[...]).astype(BF16)
        acc_ref[...] = jnp.zeros_like(acc_ref)

    @pl.when(i == 0)
    def _():
        carry_ref[j] = jnp.zeros(carry_ref.shape[1:], F32)

    xn = xn_ref[...]
    gate = _dot(xn, wg_ref[...])
    val = _dot(xn, wv_ref[...])
    gbuf_ref[0:SUBLANES, :] = carry_ref[j]
    gbuf_ref[SUBLANES:SUBLANES + tm, :] = gate
    g1 = gbuf_ref[pl.ds(SUBLANES - 1, tm), :]
    g2 = gbuf_ref[pl.ds(SUBLANES - 2, tm), :]
    t = (i * tm + lax.broadcasted_iota(jnp.int32, (tm, 1), 0)) % seq
    p1 = p1_ref[...] if has_prev else 0.0
    p2 = p2_ref[...] if has_prev else 0.0
    g1 = jnp.where(t >= 1, g1, p1)
    g2 = jnp.where(t >= 2, g2, p2)
    cw = cw_ref[...]
    conv = cb_ref[...] + cw[0:1] * g2 + cw[1:2] * g1 + cw[2:3] * gate
    h = _gelu(conv) * val
    acc_ref[...] += _dot(h.astype(BF16), wd_ref[...])
    carry_ref[j] = gate[tm - SUBLANES:tm]
    keep = cs_ref.shape[1]
    cs_ref[0] = gate[tm - keep:tm]

    @pl.when(j == pl.num_programs(1) - 1)
    def _():
        y_ref[...] = x_ref[...] + _rms(acc_ref[...]) * g4_ref[...]


def _ffn(x1, g3, wg, wv, cw, cb, wd, g4, seq, tm, prev=None):
    r, d = x1.shape
    dffp = wg.shape[1]
    tn = FFN_COL_TILE
    nj = dffp // tn
    has_prev = prev is not None
    tps = max(seq // tm, 1)
    keep = tm if has_prev else SUBLANES
    in_specs = [pl.BlockSpec((tm, d), lambda i, j: (i, 0)),
                pl.BlockSpec((1, d), lambda i, j: (0, 0)),
                pl.BlockSpec((d, tn), lambda i, j: (0, j)),
                pl.BlockSpec((d, tn), lambda i, j: (0, j)),
                pl.BlockSpec((3, tn), lambda i, j: (0, j)),
                pl.BlockSpec((1, tn), lambda i, j: (0, j)),
                pl.BlockSpec((tn, d), lambda i, j: (j, 0)),
                pl.BlockSpec((1, d), lambda i, j: (0, 0))]
    args = [x1, g3.reshape(1, d), wg, wv, cw, cb.reshape(1, dffp), wd, g4.reshape(1, d)]
    if has_prev:
        assert r == tm
        in_specs += [pl.BlockSpec((tm, tn), lambda i, j: (0, j))] * 2
        args += list(prev)
    n_cs = 1 if has_prev else r // seq
    return pl.pallas_call(
        functools.partial(_ffn_kernel, seq=seq, has_prev=has_prev, conv_w=cw.shape[0]),
        grid=(r // tm, nj), in_specs=in_specs,
        out_specs=[pl.BlockSpec((tm, d), lambda i, j: (i, 0)),
                   pl.BlockSpec((1, keep, tn), lambda i, j: (i // tps, 0, j))],
        out_shape=[jax.ShapeDtypeStruct((r, d), F32), jax.ShapeDtypeStruct((n_cs, keep, dffp), F32)],
        scratch_shapes=[pltpu.VMEM((tm, d), BF16), pltpu.VMEM((tm, d), F32),
                        pltpu.VMEM((tm + SUBLANES, tn), F32), pltpu.VMEM((nj, SUBLANES, tn), F32)],
        compiler_params=_cparams(("arbitrary", "arbitrary")), name="ffn")(*args)


def _layer(xp, xs, cache_k, cache_v, page_table, h0_re, h0_im, conv_prev, wts):
    (norm_mix_pre, norm_mix_post, norm_ffn_pre, norm_ffn_post, w_in, sb_bias, a_re, a_im, log_dt, b_re, b_im,
     c_re, c_im, d_skip, w_glu, b_glu, w_ba, w_bs, w_out, w_up, conv_w, conv_b, w_down) = wts
    bp, seq, d = xp.shape
    bs, nq, _ = xs.shape
    n_heads = sb_bias.shape[0]
    aw = n_heads * HEAD_DIM
    g, p = a_re.shape
    c = b_re.shape[-1]
    sw_total = g * c
    gb = SSM_BLOCK_CH // c
    nb = g // gb
    dff = conv_w.shape[1]
    dffp = -(-dff // FFN_COL_TILE) * FFN_COL_TILE
    widths = (aw, aw, aw, sw_total, d, d)

    w_in_b, w_glu_b, w_ba_b, w_bs_b, w_out_b = (a.astype(BF16) for a in (w_in, w_glu, w_ba, w_bs, w_out))
    padc = lambda a: jnp.pad(a, ((0, 0), (0, dffp - dff)))
    wg_b = padc(w_up[:, :dff]).astype(BF16)
    wv_b = padc(w_up[:, dff:]).astype(BF16)
    wd_b = jnp.pad(w_down, ((0, dffp - dff), (0, 0))).astype(BF16)
    cw_p, cb_p = padc(conv_w), jnp.pad(conv_b, (0, dffp - dff))

    powr, powi, bbr, bbi = _ssm_prep(a_re, a_im, log_dt, b_re, b_im)
    bblk, cblk = _ssm_block_weights(bbr, bbi, c_re, c_im)

    def states(h):
        h = h.reshape(-1, nb, 2, gb, p)
        return h[:, :, 0].reshape(-1, g, p), h[:, :, 1].reshape(-1, g, p)

    rp = bp * seq
    x2 = xp.reshape(rp, d)
    q, k, v, u, ga, gs = _in_proj(x2, norm_mix_pre, w_in_b, widths, _pick_tile(rp, 512))
    o_attn = _attn_prompt(q, k, v, sb_bias, bp, seq)
    tc = _pick_tile(seq, 512)
    gg, hs = _ssm(u, bblk, cblk, _scan_consts(powr, powi, SUBLANES, gb), d_skip, seq, tc)
    hrp, hip = states(hs[:, :, SUBLANES - 1, :].reshape(bp, -1))
    x1 = _mix(o_attn, gg, ga, gs, x2, w_glu_b, b_glu, w_ba_b, w_bs_b, w_out_b, norm_mix_post, _pick_tile(rp, 256))
    tmf = _pick_tile(seq, 512)
    yp, csp = _ffn(x1, norm_ffn_pre, wg_b, wv_b, cw_p, cb_p, wd_b, norm_ffn_post, seq, tmf)
    out_p = (yp.reshape(bp, seq, d), k.reshape(bp, seq, n_heads, HEAD_DIM), v.reshape(bp, seq, n_heads, HEAD_DIM),
             hrp, hip, csp[:, SUBLANES - (conv_w.shape[0] - 1):, :dff])

    rs = bs * nq
    assert SUBLANES % nq == 0
    xs2 = xs.reshape(rs, d)
    q, k, v, u, ga, gs = _in_proj(xs2, norm_mix_pre, w_in_b, widths, rs)
    o_attn = _attn_sample(q, k, v, sb_bias, cache_k, cache_v, page_table, bs, nq)
    h0 = jnp.stack([h0_re.reshape(bs, nb, gb * p), h0_im.reshape(bs, nb, gb * p)], axis=2).reshape(bs, nb * 2 * gb * p)
    gg, hs = _ssm(u, bblk, cblk, _scan_consts(powr, powi, nq, gb), d_skip, nq, rs, h0_rows=jnp.repeat(h0, nq, axis=0))
    hrs, his = states(hs.reshape(bs, nq, -1)[:, nq - 1])
    x1 = _mix(o_attn, gg, ga, gs, xs2, w_glu_b, b_glu, w_ba_b, w_bs_b, w_out_b, norm_mix_post, rs)
    t_row = jnp.arange(rs) % nq
    prev = padc(conv_prev.reshape(bs * (conv_w.shape[0] - 1), dff)).reshape(bs, conv_w.shape[0] - 1, dffp)
    prev_rows = jnp.repeat(prev, nq, axis=0)
    p1 = jnp.where((t_row == 0)[:, None], prev_rows[:, 1], 0.0)
    p2 = jnp.where((t_row == 0)[:, None], prev_rows[:, 0], jnp.where((t_row == 1)[:, None], prev_rows[:, 1], 0.0))
    ys, css = _ffn(x1, norm_ffn_pre, wg_b, wv_b, cw_p, cb_p, wd_b, norm_ffn_post, nq, rs, prev=(p1, p2))
    css = css.reshape(bs, nq, dffp)[:, nq - (conv_w.shape[0] - 1):, :dff]
    out_s = (ys.reshape(bs, nq, d), k.reshape(bs, nq, n_heads, HEAD_DIM), v.reshape(bs, nq, n_heads, HEAD_DIM),
             hrs, his, css)
    return out_p, out_s


def kernel(x_prompt, x_sample, cache_k, cache_v, page_table, state_ssm_re, state_ssm_im, state_ffn_conv, norm_mix_pre, norm_mix_post, norm_ffn_pre, norm_ffn_post, w_in, sb_bias, ssm_a_re, ssm_a_im, ssm_log_dt, ssm_b_re, ssm_b_im, ssm_c_re, ssm_c_im, ssm_d, ssm_w_glu, ssm_b_glu, w_branch_attn, w_branch_ssm, w_out, w_up, conv_w, conv_b, w_down):
    depth = w_in.shape[0]
    xp, xs = x_prompt, x_sample
    acc_p, acc_s = [], []
    for layer in range(depth):
        wts = tuple(a[layer] for a in (
            norm_mix_pre, norm_mix_post, norm_ffn_pre, norm_ffn_post, w_in, sb_bias, ssm_a_re, ssm_a_im, ssm_log_dt,
            ssm_b_re, ssm_b_im, ssm_c_re, ssm_c_im, ssm_d, ssm_w_glu, ssm_b_glu, w_branch_attn, w_branch_ssm, w_out,
            w_up, conv_w, conv_b, w_down))
        out_p, out_s = _layer(xp, xs, cache_k[layer], cache_v[layer], page_table, state_ssm_re[layer],
                              state_ssm_im[layer], state_ffn_conv[layer], wts)
        xp, xs = out_p[0], out_s[0]
        acc_p.append(out_p[1:])
        acc_s.append(out_s[1:])
    stack = lambda acc, n: jnp.stack([a[n] for a in acc])
    kp, vp, hrp, hip, cp = (stack(acc_p, n) for n in range(5))
    ks, vs, hrs, his, cs = (stack(acc_s, n) for n in range(5))
    return (xp, xs, kp, vp, ks, vs, hrp, hip, hrs, his, cp, cs)
```

```python
import functools
import math

import jax
import jax.numpy as jnp
from jax import lax
from jax.experimental import pallas as pl
from jax.experimental.pallas import tpu as pltpu

RMS_EPS = 1e-6
LOG2E = math.log2(math.e)
HEAD_DIM = 64
HEADS_PER_BLOCK = 2
LANES = 128
SUBLANES = 8
SSM_BLOCK_CH = 128
VMEM_LIMIT = 56 * 1024 * 1024
ATTN_Q_TILE = 512
ATTN_K_TILE = 256
FFN_COL_TILE = 512
BF16 = jnp.bfloat16
F32 = jnp.float32


def _cparams(sem):
    return pltpu.CompilerParams(dimension_semantics=sem, vmem_limit_bytes=VMEM_LIMIT)


def _pick_tile(n, target, quantum=SUBLANES):
    best = None
    for t in range(quantum, min(n, target) + 1, quantum):
        if n % t == 0:
            best = t
    assert best is not None, (n, target, quantum)
    return best


def _dot(a, b):
    return jnp.dot(a, b, preferred_element_type=F32)


def _dot_nt(a, b):
    return lax.dot_general(a, b, (((1,), (1,)), ((), ())), preferred_element_type=F32)


def _sigmoid(x):
    return 1.0 / (1.0 + jnp.exp(-x))


def _gelu(x):
    c = math.sqrt(2.0 / math.pi)
    return 0.5 * x * (1.0 + jnp.tanh(c * (x + 0.044715 * (x * x * x))))


def _rms(x):
    return x * lax.rsqrt(jnp.mean(x * x, axis=-1, keepdims=True) + RMS_EPS)


def _softplus(z):
    return jnp.maximum(z, 0.0) + jnp.log(1.0 + jnp.exp2(jnp.abs(z) * (-LOG2E)))


def _ssm_discretise(ar, ai, ldt):
    dt = jnp.exp(ldt)
    mag = jnp.exp(dt * ar)
    return mag * jnp.cos(dt * ai), mag * jnp.sin(dt * ai)


def _ssm_prep_kernel(ar_ref, ai_ref, ldt_ref, ar3_ref, ai3_ref, ldt3_ref, br_ref, bi_ref,
                     powr_ref, powi_ref, bbr_ref, bbi_ref):
    ar, ai = ar3_ref[...], ai3_ref[...]
    abr, abi = _ssm_discretise(ar, ai, ldt3_ref[...])
    nr, ni = abr - 1.0, abi
    den = ar * ar + ai * ai
    fr, fi = (nr * ar + ni * ai) / den, (ni * ar - nr * ai) / den
    br, bi = br_ref[...], bi_ref[...]
    bbr_ref[...] = fr * br - fi * bi
    bbi_ref[...] = fr * bi + fi * br
    abr, abi = _ssm_discretise(ar_ref[...], ai_ref[...], ldt_ref[...])
    pr, pi = abr, abi
    powr_ref[0], powi_ref[0] = pr, pi
    for k in range(1, SUBLANES):
        pr, pi = pr * abr - pi * abi, pr * abi + pi * abr
        powr_ref[k], powi_ref[k] = pr, pi


def _ssm_prep(a_re, a_im, log_dt, b_re, b_im):
    g, p = a_re.shape
    c = b_re.shape[-1]
    out = (jax.ShapeDtypeStruct((SUBLANES, g, p), F32),) * 2 + (jax.ShapeDtypeStruct((g, c, p), F32),) * 2
    return pl.pallas_call(_ssm_prep_kernel, out_shape=out, name="ssm_prep")(
        a_re, a_im, log_dt.reshape(g, 1), a_re.reshape(g, 1, p), a_im.reshape(g, 1, p), log_dt.reshape(g, 1, 1),
        b_re.transpose(0, 2, 1), b_im.transpose(0, 2, 1))


def _ssm_block_weights(bbr, bbi, c_re, c_im):
    g, c, p = bbr.shape
    gb = SSM_BLOCK_CH // c
    nb = g // gb
    eye = jnp.eye(gb, dtype=F32)
    bb = jnp.stack([bbr, bbi], axis=0).reshape(2, nb, gb, c, p)
    bblk = jnp.einsum("rjgcp,gh->jgcrhp", bb, eye).reshape(nb, gb * c, 2 * gb * p)
    cc = jnp.stack([c_re, -c_im], axis=0).reshape(2, nb, gb, c, p)
    cblk = jnp.einsum("rjgcp,gh->jrgphc", cc, eye).reshape(nb, 2 * gb * p, gb * c)
    return bblk.astype(BF16), cblk.astype(BF16)


def _scan_consts(powr, powi, seg, gb):
    _, g, p = powr.shape
    nb = g // gb
    rr = jnp.arange(SUBLANES) % seg
    kinds = []
    for d in (1, 2, 4):
        m = (rr >= d).astype(F32)[:, None, None]
        kinds += [m * powr[d - 1][None], m * powi[d - 1][None]]
    kinds += [powr[rr], powi[rr]]
    k = jnp.stack(kinds, axis=0)
    return k.reshape(8, SUBLANES, nb, gb * p).transpose(2, 0, 1, 3)


def _inproj_kernel(x_ref, g_ref, w_ref, *rest, ranges):
    out_refs, xn_ref = rest[:-1], rest[-1]
    j = pl.program_id(1)

    @pl.when(j == 0)
    def _():
        xn_ref[...] = (_rms(x_ref[...]) * g_ref[...]).astype(BF16)

    acc = _dot(xn_ref[...], w_ref[...])
    for ref, (lo, hi) in zip(out_refs, ranges):
        @pl.when((j >= lo) & (j < hi))
        def _(ref=ref):
            ref[...] = acc


def _in_proj(x, g, w_bf16, widths, tm):
    r, d = x.shape
    n = w_bf16.shape[1]
    tn = 512
    ranges, specs, shapes, lo = [], [], [], 0
    for wd in widths:
        nblk = wd // tn
        ranges.append((lo, lo + nblk))
        specs.append(pl.BlockSpec((tm, tn), lambda i, j, lo=lo, nblk=nblk: (i, jnp.clip(j - lo, 0, nblk - 1))))
        shapes.append(jax.ShapeDtypeStruct((r, wd), F32))
        lo += nblk
    assert lo * tn == n
    return pl.pallas_call(
        functools.partial(_inproj_kernel, ranges=tuple(ranges)),
        grid=(r // tm, n // tn),
        in_specs=[pl.BlockSpec((tm, d), lambda i, j: (i, 0)),
                  pl.BlockSpec((1, d), lambda i, j: (0, 0)),
                  pl.BlockSpec((d, tn), lambda i, j: (0, j))],
        out_specs=specs, out_shape=shapes,
        scratch_shapes=[pltpu.VMEM((tm, d), BF16)],
        compiler_params=_cparams(("arbitrary", "arbitrary")), name="in_proj")(x, g.reshape(1, d), w_bf16)


def _sb_weights(z, mask, tail, tri2):
    nk = tri2.shape[1]
    sp = _softplus(z)
    if mask is not None:
        sp = jnp.where(mask, sp, 0.0)
    hi = sp.astype(BF16)
    lo = (sp - hi.astype(F32)).astype(BF16)
    parts = []
    for c in range(z.shape[1] // nk):
        cols = slice(c * nk, (c + 1) * nk)
        local = _dot(jnp.concatenate([hi[:, cols], lo[:, cols]], axis=1), tri2)
        parts.append(local + tail)
        tail = tail + local[:, 0:1]
    rsum = parts[0] if len(parts) == 1 else jnp.concatenate(parts, axis=1)
    w = jnp.exp(z - rsum)
    if mask is not None:
        w = jnp.where(mask, w, 0.0)
    return w.astype(BF16), tail


def _tri2(n):
    tri = (jnp.arange(n)[:, None] >= jnp.arange(n)[None, :]).astype(BF16)
    return jnp.concatenate([tri, tri], axis=0)


def _attn_prompt_kernel(bias_ref, q_ref, k_ref, v_ref, tri_ref, o_ref, kb_ref, vb_ref, acc_ref, *, tq, tk):
    hp, qi = pl.program_id(1), pl.program_id(2)
    nd = tq // tk

    @pl.when(qi == 0)
    def _():
        kb_ref[...] = k_ref[...].astype(BF16)
        vb_ref[...] = v_ref[...].astype(BF16)

    lane = lax.broadcasted_iota(jnp.int32, (1, LANES), 1)
    first = lane < HEAD_DIM
    q = q_ref[...] * (HEAD_DIM ** -0.5)
    qh = (jnp.where(first, q, 0.0).astype(BF16), jnp.where(first, 0.0, q).astype(BF16))
    bias = (bias_ref[HEADS_PER_BLOCK * hp], bias_ref[HEADS_PER_BLOCK * hp + 1])
    tri2 = tri_ref[...]
    qpos = qi * tq + lax.broadcasted_iota(jnp.int32, (tq, tk), 0)
    kcol = lax.broadcasted_iota(jnp.int32, (tq, tk), 1)

    def block(j, h, tail, masked):
        start = pl.multiple_of(j * tk, tk)
        z = _dot_nt(qh[h], kb_ref[pl.ds(start, tk), :]) + bias[h]
        mask = (j * tk + kcol < qpos) if masked else None
        w, tail = _sb_weights(z, mask, tail, tri2)
        return _dot(w, vb_ref[pl.ds(start, tk), :]), tail

    tails = [jnp.zeros((tq, 1), F32)] * HEADS_PER_BLOCK
    for dd in reversed(range(nd)):
        for h in range(HEADS_PER_BLOCK):
            pv, tails[h] = block(qi * nd + dd, h, tails[h], True)
            if dd == nd - 1:
                acc_ref[h] = pv
            else:
                acc_ref[h] += pv

    def body(it, tails):
        j = qi * nd - 1 - it
        new = []
        for h in range(HEADS_PER_BLOCK):
            pv, tl = block(j, h, tails[h], False)
            acc_ref[h] += pv
            new.append(tl)
        return tuple(new)

    lax.fori_loop(0, qi * nd, body, tuple(tails))
    o_ref[...] = jnp.where(first, acc_ref[0], acc_ref[1])


def _attn_prompt(q, k, v, sb_bias, bsz, seq):
    r, width = q.shape
    tk = _pick_tile(seq, ATTN_K_TILE, LANES)
    tq = _pick_tile(seq, ATTN_Q_TILE, tk)
    nq = seq // tq
    nhp = width // LANES
    return pl.pallas_call(
        functools.partial(_attn_prompt_kernel, tq=tq, tk=tk),
        grid=(bsz, nhp, nq),
        in_specs=[pl.BlockSpec(memory_space=pltpu.SMEM),
                  pl.BlockSpec((tq, LANES), lambda b, hp, qi: (b * nq + qi, hp)),
                  pl.BlockSpec((seq, LANES), lambda b, hp, qi: (b, hp)),
                  pl.BlockSpec((seq, LANES), lambda b, hp, qi: (b, hp)),
                  pl.BlockSpec((2 * tk, tk), lambda b, hp, qi: (0, 0))],
        out_specs=pl.BlockSpec((tq, LANES), lambda b, hp, qi: (b * nq + qi, hp)),
        out_shape=jax.ShapeDtypeStruct((r, width), F32),
        scratch_shapes=[pltpu.VMEM((seq, LANES), BF16), pltpu.VMEM((seq, LANES), BF16),
                        pltpu.VMEM((HEADS_PER_BLOCK, tq, LANES), F32)],
        compiler_params=_cparams(("arbitrary", "arbitrary", "arbitrary")), name="attn_prompt")(
            sb_bias, q, k, v, _tri2(tk))


def _attn_sample_kernel(pt_ref, q_ref, kn_ref, vn_ref, bias_ref, tri_ref, *rest, n_heads, n_q, pages_per_step):
    del pt_ref
    k_refs = rest[:pages_per_step]
    v_refs = rest[pages_per_step:2 * pages_per_step]
    o_ref, wq_ref, acc_ref, tail_ref, pad_ref = rest[2 * pages_per_step:]
    s = pl.program_id(1)
    nrow = n_q * n_heads
    width = n_heads * HEAD_DIM
    assert n_heads & (n_heads - 1) == 0 and HEAD_DIM & (HEAD_DIM - 1) == 0
    head_shift, dim_shift = n_heads.bit_length() - 1, HEAD_DIM.bit_length() - 1
    r_head = lax.broadcasted_iota(jnp.int32, (nrow, width), 0) & (n_heads - 1)
    c_head = lax.broadcasted_iota(jnp.int32, (nrow, width), 1) >> dim_shift
    own_head = r_head == c_head
    page = tri_ref.shape[1]
    tri2 = tri_ref[...]
    bias = bias_ref[...]

    @pl.when(s == 0)
    def _():
        q = q_ref[0] * (HEAD_DIM ** -0.5)
        rows = jnp.concatenate([jnp.broadcast_to(q[i:i + 1], (n_heads, width)) for i in range(n_q)], axis=0)
        wq_ref[...] = jnp.where(own_head, rows, 0.0).astype(BF16)
        pad_ref[...] = jnp.zeros_like(pad_ref)
        pad_ref[0:SUBLANES, :] = kn_ref[0]
        z = _dot_nt(wq_ref[...], pad_ref[...].astype(BF16)) + bias
        pad_ref[0:SUBLANES, :] = vn_ref[0]
        kpos = lax.broadcasted_iota(jnp.int32, (nrow, page), 1)
        qpos = lax.broadcasted_iota(jnp.int32, (nrow, page), 0) >> head_shift
        w, tl = _sb_weights(z, kpos < qpos, jnp.zeros((nrow, 1), F32), tri2)
        acc_ref[...] = _dot(w, pad_ref[...].astype(BF16))
        tail_ref[...] = tl

    kcat = jnp.concatenate([r[0].astype(BF16) for r in k_refs], axis=1)
    vcat = jnp.concatenate([r[0].astype(BF16) for r in v_refs], axis=1)
    z = _dot(wq_ref[...], kcat) + bias
    w, tail = _sb_weights(z, None, tail_ref[...], tri2)
    acc = acc_ref[...] + _dot_nt(w, vcat)
    acc_ref[...] = acc
    tail_ref[...] = tail

    @pl.when(s == pl.num_programs(1) - 1)
    def _():
        own = jnp.where(own_head, acc, 0.0)
        o_ref[0] = jnp.concatenate(
            [jnp.sum(own[i * n_heads:(i + 1) * n_heads], axis=0, keepdims=True) for i in range(n_q)], axis=0)


def _attn_sample(q, k_new, v_new, sb_bias, cache_k, cache_v, page_table, bs, n_q):
    n_pool, page, n_heads, hd = cache_k.shape
    width = n_heads * hd
    n_pages = page_table.shape[1]
    pps = _pick_tile(n_pages, 8, 1)
    nrow = n_q * n_heads
    ck = cache_k.transpose(0, 2, 3, 1).reshape(n_pool, width, page)
    cv = cache_v.transpose(0, 2, 3, 1).reshape(n_pool, width, page)
    pad = lambda a: jnp.pad(a.reshape(bs, n_q, width), ((0, 0), (0, SUBLANES - n_q), (0, 0)))
    bias_col = jnp.tile(sb_bias, n_q).reshape(nrow, 1)
    tri = _tri2(page)

    def page_spec(p):
        return pl.BlockSpec((1, width, page),
                            lambda b, s, pt, p=p: (pt[b, n_pages - 1 - (s * pps + p)], 0, 0))

    grid_spec = pltpu.PrefetchScalarGridSpec(
        num_scalar_prefetch=1, grid=(bs, n_pages // pps),
        in_specs=[pl.BlockSpec((1, n_q, width), lambda b, s, pt: (b, 0, 0)),
                  pl.BlockSpec((1, SUBLANES, width), lambda b, s, pt: (b, 0, 0)),
                  pl.BlockSpec((1, SUBLANES, width), lambda b, s, pt: (b, 0, 0)),
                  pl.BlockSpec((nrow, 1), lambda b, s, pt: (0, 0)),
                  pl.BlockSpec((2 * page, page), lambda b, s, pt: (0, 0))]
                 + [page_spec(p) for p in range(pps)] * 2,
        out_specs=pl.BlockSpec((1, n_q, width), lambda b, s, pt: (b, 0, 0)),
        scratch_shapes=[pltpu.VMEM((nrow, width), BF16), pltpu.VMEM((nrow, width), F32),
                        pltpu.VMEM((nrow, 1), F32), pltpu.VMEM((page, width), F32)])
    out = pl.pallas_call(
        functools.partial(_attn_sample_kernel, n_heads=n_heads, n_q=n_q, pages_per_step=pps),
        grid_spec=grid_spec, out_shape=jax.ShapeDtypeStruct((bs, n_q, width), F32),
        compiler_params=_cparams(("arbitrary", "arbitrary")), name="attn_sample")(
            page_table, q.reshape(bs, n_q, width), pad(k_new), pad(v_new), bias_col, tri,
            *([ck] * pps), *([cv] * pps))
    return out.reshape(bs * n_q, width)


def _ssm_kernel(*refs, chunks_per_seq, per_row_h0):
    if per_row_h0:
        u_ref, bw_ref, cw_ref, k_ref, d_ref, h0_ref, gg_ref, hs_ref, bu_ref = refs
    else:
        u_ref, bw_ref, cw_ref, k_ref, d_ref, gg_ref, hs_ref, bu_ref, carry_ref = refs
    i, j = pl.program_id(0), pl.program_id(1)
    rows = u_ref.shape[0]
    half = bu_ref.shape[1] // 2
    u = u_ref[...]
    bu_ref[...] = _dot(u.astype(BF16), bw_ref[0])

    if not per_row_h0:
        @pl.when(i % chunks_per_seq == 0)
        def _():
            carry_ref[j] = jnp.zeros((1, 2 * half), F32)

    def tile(tix, carry):
        r0 = pl.multiple_of(tix * SUBLANES, SUBLANES)
        br = bu_ref[pl.ds(r0, SUBLANES), 0:half]
        bi = bu_ref[pl.ds(r0, SUBLANES), half:2 * half]
        for n, d in enumerate((1, 2, 4)):
            ar, ai = k_ref[0, 2 * n], k_ref[0, 2 * n + 1]
            sr, si = pltpu.roll(br, d, 0), pltpu.roll(bi, d, 0)
            br, bi = br + ar * sr - ai * si, bi + ar * si + ai * sr
        pr, pi = k_ref[0, 6], k_ref[0, 7]
        if per_row_h0:
            cr = h0_ref[pl.ds(r0, SUBLANES), 0:half]
            ci = h0_ref[pl.ds(r0, SUBLANES), half:2 * half]
        else:
            cr, ci = carry
        hr = br + pr * cr - pi * ci
        hi = bi + pr * ci + pi * cr
        bu_ref[pl.ds(r0, SUBLANES), 0:half] = hr
        bu_ref[pl.ds(r0, SUBLANES), half:2 * half] = hi
        return (hr[SUBLANES - 1:SUBLANES], hi[SUBLANES - 1:SUBLANES])

    if per_row_h0:
        init = (jnp.zeros((1, half), F32),) * 2
    else:
        init = (carry_ref[j, :, 0:half], carry_ref[j, :, half:2 * half])
    last = lax.fori_loop(0, rows // SUBLANES, tile, init)
    if per_row_h0:
        hs_ref[...] = bu_ref[...]
    else:
        carry_ref[j, :, 0:half] = last[0]
        carry_ref[j, :, half:2 * half] = last[1]
        hs_ref[0, 0] = bu_ref[rows - SUBLANES:rows, :]
    y = _dot(bu_ref[...].astype(BF16), cw_ref[0]) + d_ref[...] * u
    gg_ref[...] = _gelu(y)


def _ssm(u, bblk, cblk, consts, d_skip, seq, tc, h0_rows=None):
    r, width = u.shape
    nb, cb, sw = bblk.shape
    per_row = h0_rows is not None
    cps = max(seq // tc, 1)
    in_specs = [pl.BlockSpec((tc, cb), lambda i, j: (i, j)),
                pl.BlockSpec((1, cb, sw), lambda i, j: (j, 0, 0)),
                pl.BlockSpec((1, sw, cb), lambda i, j: (j, 0, 0)),
                pl.BlockSpec((1, 8, SUBLANES, sw // 2), lambda i, j: (j, 0, 0, 0)),
                pl.BlockSpec((1, cb), lambda i, j: (0, j))]
    args = [u, bblk, cblk, consts, d_skip.reshape(1, width)]
    scratch = [pltpu.VMEM((tc, sw), F32)]
    if per_row:
        assert r == tc
        in_specs.append(pl.BlockSpec((tc, sw), lambda i, j: (0, j)))
        args.append(h0_rows)
        hs_shape = jax.ShapeDtypeStruct((r, nb * sw), F32)
        hs_spec = pl.BlockSpec((tc, sw), lambda i, j: (0, j))
    else:
        hs_shape = jax.ShapeDtypeStruct((r // tc, nb, SUBLANES, sw), F32)
        hs_spec = pl.BlockSpec((1, 1, SUBLANES, sw), lambda i, j: (i, j, 0, 0))
        scratch.append(pltpu.VMEM((nb, 1, sw), F32))
    return pl.pallas_call(
        functools.partial(_ssm_kernel, chunks_per_seq=cps, per_row_h0=per_row),
        grid=(r // tc, nb), in_specs=in_specs,
        out_specs=[pl.BlockSpec((tc, cb), lambda i, j: (i, j)), hs_spec],
        out_shape=[jax.ShapeDtypeStruct((r, width), F32), hs_shape],
        scratch_shapes=scratch,
        compiler_params=_cparams(("arbitrary", "arbitrary")), name="ssm")(*args)


def _mix_kernel(oa_ref, gg_ref, ga_ref, gs_ref, x_ref, wglu_ref, bglu_ref, wba_ref, wbs_ref, wout_ref, g_ref, x1_ref):
    gg = gg_ref[...]
    o_ssm = gg * _sigmoid(_dot(gg.astype(BF16), wglu_ref[...]) + bglu_ref[...])
    merged = (_sigmoid(ga_ref[...]) * _dot(oa_ref[...].astype(BF16), wba_ref[...])
              + _sigmoid(gs_ref[...]) * _dot(o_ssm.astype(BF16), wbs_ref[...]))
    mo = _dot(merged.astype(BF16), wout_ref[...])
    x1_ref[...] = x_ref[...] + _rms(mo) * g_ref[...]


def _mix(o_attn, gg, ga, gs, x, wglu, bglu, wba, wbs, wout, g, tm):
    r, d = x.shape
    w = o_attn.shape[1]
    row = lambda n: pl.BlockSpec((tm, n), lambda i: (i, 0))
    const = lambda a, b: pl.BlockSpec((a, b), lambda i: (0, 0), pipeline_mode=pl.Buffered(1))
    return pl.pallas_call(
        _mix_kernel, grid=(r // tm,),
        in_specs=[row(w), row(w), row(d), row(d), row(d), const(w, w), const(1, w), const(w, d), const(w, d),
                  const(d, d), const(1, d)],
        out_specs=row(d), out_shape=jax.ShapeDtypeStruct((r, d), F32),
        compiler_params=_cparams(("arbitrary",)), name="mix")(
            o_attn, gg, ga, gs, x, wglu, bglu.reshape(1, w), wba, wbs, wout, g.reshape(1, d))


def _ffn_kernel(*refs, seq, has_prev, conv_w):
    if has_prev:
        (x_ref, g3_ref, wg_ref, wv_ref, cw_ref, cb_ref, wd_ref, g4_ref, p1_ref, p2_ref,
         y_ref, cs_ref, xn_ref, acc_ref, gbuf_ref, carry_ref) = refs
    else:
        (x_ref, g3_ref, wg_ref, wv_ref, cw_ref, cb_ref, wd_ref, g4_ref,
         y_ref, cs_ref, xn_ref, acc_ref, gbuf_ref, carry_ref) = refs
    i, j = pl.program_id(0), pl.program_id(1)
    tm = x_ref.shape[0]
    assert conv_w == 3

    @pl.when(j == 0)
    def _():
        xn_ref[...] = (_rms(x_ref[...]) * g3_ref[...]).astype(BF16)
        acc_ref[...] = jnp.zeros_like(acc_ref)

    @pl.when(i == 0)
    def _():
        carry_ref[j] = jnp.zeros(carry_ref.shape[1:], F32)

    xn = xn_ref[...]
    gate = _dot(xn, wg_ref[...])
    val = _dot(xn, wv_ref[...])
    gbuf_ref[0:SUBLANES, :] = carry_ref[j]
    gbuf_ref[SUBLANES:SUBLANES + tm, :] = gate
    g1 = gbuf_ref[pl.ds(SUBLANES - 1, tm), :]
    g2 = gbuf_ref[pl.ds(SUBLANES - 2, tm), :]
    t = (i * tm + lax.broadcasted_iota(jnp.int32, (tm, 1), 0)) % seq
    p1 = p1_ref[...] if has_prev else 0.0
    p2 = p2_ref[...] if has_prev else 0.0
    g1 = jnp.where(t >= 1, g1, p1)
    g2 = jnp.where(t >= 2, g2, p2)
    cw = cw_ref[...]
    conv = cb_ref[...] + cw[0:1] * g2 + cw[1:2] * g1 + cw[2:3] * gate
    h = _gelu(conv) * val
    acc_ref[...] += _dot(h.astype(BF16), wd_ref[...])
    carry_ref[j] = gate[tm - SUBLANES:tm]
    keep = cs_ref.shape[1]
    cs_ref[0] = gate[tm - keep:tm]

    @pl.when(j == pl.num_programs(1) - 1)
    def _():
        y_ref[...] = x_ref[...] + _rms(acc_ref[...]) * g4_ref[...]


def _ffn(x1, g3, wg, wv, cw, cb, wd, g4, seq, tm, prev=None):
    r, d = x1.shape
    dffp = wg.shape[1]
    tn = FFN_COL_TILE
    nj = dffp // tn
    has_prev = prev is not None
    keep = tm if has_prev else SUBLANES
    in_specs = [pl.BlockSpec((tm, d), lambda i, j: (i, 0)),
                pl.BlockSpec((1, d), lambda i, j: (0, 0)),
                pl.BlockSpec((d, tn), lambda i, j: (0, j)),
                pl.BlockSpec((d, tn), lambda i, j: (0, j)),
                pl.BlockSpec((3, tn), lambda i, j: (0, j)),
                pl.BlockSpec((1, tn), lambda i, j: (0, j)),
                pl.BlockSpec((tn, d), lambda i, j: (j, 0)),
                pl.BlockSpec((1, d), lambda i, j: (0, 0))]
    args = [x1, g3.reshape(1, d), wg, wv, cw, cb.reshape(1, dffp), wd, g4.reshape(1, d)]
    if has_prev:
        assert r == tm
        in_specs += [pl.BlockSpec((tm, tn), lambda i, j: (0, j))] * 2
        args += list(prev)
    n_cs = r // tm
    return pl.pallas_call(
        functools.partial(_ffn_kernel, seq=seq, has_prev=has_prev, conv_w=cw.shape[0]),
        grid=(r // tm, nj), in_specs=in_specs,
        out_specs=[pl.BlockSpec((tm, d), lambda i, j: (i, 0)),
                   pl.BlockSpec((1, keep, tn), lambda i, j: (i, 0, j))],
        out_shape=[jax.ShapeDtypeStruct((r, d), F32), jax.ShapeDtypeStruct((n_cs, keep, dffp), F32)],
        scratch_shapes=[pltpu.VMEM((tm, d), BF16), pltpu.VMEM((tm, d), F32),
                        pltpu.VMEM((tm + SUBLANES, tn), F32), pltpu.VMEM((nj, SUBLANES, tn), F32)],
        compiler_params=_cparams(("arbitrary", "arbitrary")), name="ffn")(*args)


def _layer(xp, xs, cache_k, cache_v, page_table, h0_re, h0_im, conv_prev, wts):
    (norm_mix_pre, norm_mix_post, norm_ffn_pre, norm_ffn_post, w_in, sb_bias, a_re, a_im, log_dt, b_re, b_im,
     c_re, c_im, d_skip, w_glu, b_glu, w_ba, w_bs, w_out, w_up, conv_w, conv_b, w_down) = wts
    bp, seq, d = xp.shape
    bs, nq, _ = xs.shape
    n_heads = sb_bias.shape[0]
    aw = n_heads * HEAD_DIM
    g, p = a_re.shape
    c = b_re.shape[-1]
    sw_total = g * c
    gb = SSM_BLOCK_CH // c
    nb = g // gb
    dff = conv_w.shape[1]
    dffp = -(-dff // FFN_COL_TILE) * FFN_COL_TILE
    widths = (aw, aw, aw, sw_total, d, d)

    w_in_b, w_glu_b, w_ba_b, w_bs_b, w_out_b = (a.astype(BF16) for a in (w_in, w_glu, w_ba, w_bs, w_out))
    padc = lambda a: jnp.pad(a, ((0, 0), (0, dffp - dff)))
    wg_b = padc(w_up[:, :dff]).astype(BF16)
    wv_b = padc(w_up[:, dff:]).astype(BF16)
    wd_b = jnp.pad(w_down, ((0, dffp - dff), (0, 0))).astype(BF16)
    cw_p, cb_p = padc(conv_w), jnp.pad(conv_b, (0, dffp - dff))

    powr, powi, bbr, bbi = _ssm_prep(a_re, a_im, log_dt, b_re, b_im)
    bblk, cblk = _ssm_block_weights(bbr, bbi, c_re, c_im)

    def states(h):
        h = h.reshape(-1, nb, 2, gb, p)
        return h[:, :, 0].reshape(-1, g, p), h[:, :, 1].reshape(-1, g, p)

    rp = bp * seq
    x2 = xp.reshape(rp, d)
    q, k, v, u, ga, gs = _in_proj(x2, norm_mix_pre, w_in_b, widths, _pick_tile(rp, 512))
    o_attn = _attn_prompt(q, k, v, sb_bias, bp, seq)
    tc = _pick_tile(seq, 512)
    gg, hs = _ssm(u, bblk, cblk, _scan_consts(powr, powi, SUBLANES, gb), d_skip, seq, tc)
    hrp, hip = states(hs.reshape(bp, seq // tc, nb, SUBLANES, -1)[:, -1, :, SUBLANES - 1, :].reshape(bp, -1))
    x1 = _mix(o_attn, gg, ga, gs, x2, w_glu_b, b_glu, w_ba_b, w_bs_b, w_out_b, norm_mix_post, _pick_tile(rp, 256))
    tmf = _pick_tile(seq, 512)
    yp, csp = _ffn(x1, norm_ffn_pre, wg_b, wv_b, cw_p, cb_p, wd_b, norm_ffn_post, seq, tmf)
    out_p = (yp.reshape(bp, seq, d), k.reshape(bp, seq, n_heads, HEAD_DIM), v.reshape(bp, seq, n_heads, HEAD_DIM),
             hrp, hip, csp.reshape(bp, seq // tmf, SUBLANES, dffp)[:, -1, SUBLANES - (conv_w.shape[0] - 1):, :dff])

    rs = bs * nq
    assert SUBLANES % nq == 0
    xs2 = xs.reshape(rs, d)
    q, k, v, u, ga, gs = _in_proj(xs2, norm_mix_pre, w_in_b, widths, rs)
    o_attn = _attn_sample(q, k, v, sb_bias, cache_k, cache_v, page_table, bs, nq)
    h0 = jnp.stack([h0_re.reshape(bs, nb, gb * p), h0_im.reshape(bs, nb, gb * p)], axis=2).reshape(bs, nb * 2 * gb * p)
    gg, hs = _ssm(u, bblk, cblk, _scan_consts(powr, powi, nq, gb), d_skip, nq, rs, h0_rows=jnp.repeat(h0, nq, axis=0))
    hrs, his = states(hs.reshape(bs, nq, -1)[:, nq - 1])
    x1 = _mix(o_attn, gg, ga, gs, xs2, w_glu_b, b_glu, w_ba_b, w_bs_b, w_out_b, norm_mix_post, rs)
    t_row = jnp.arange(rs) % nq
    prev = padc(conv_prev.reshape(bs * (conv_w.shape[0] - 1), dff)).reshape(bs, conv_w.shape[0] - 1, dffp)
    prev_rows = jnp.repeat(prev, nq, axis=0)
    p1 = jnp.where((t_row == 0)[:, None], prev_rows[:, 1], 0.0)
    p2 = jnp.where((t_row == 0)[:, None], prev_rows[:, 0], jnp.where((t_row == 1)[:, None], prev_rows[:, 1], 0.0))
    ys, css = _ffn(x1, norm_ffn_pre, wg_b, wv_b, cw_p, cb_p, wd_b, norm_ffn_post, nq, rs, prev=(p1, p2))
    css = css.reshape(bs, nq, dffp)[:, nq - (conv_w.shape[0] - 1):, :dff]
    out_s = (ys.reshape(bs, nq, d), k.reshape(bs, nq, n_heads, HEAD_DIM), v.reshape(bs, nq, n_heads, HEAD_DIM),
             hrs, his, css)
    return out_p, out_s


def kernel(x_prompt, x_sample, cache_k, cache_v, page_table, state_ssm_re, state_ssm_im, state_ffn_conv, norm_mix_pre, norm_mix_post, norm_ffn_pre, norm_ffn_post, w_in, sb_bias, ssm_a_re, ssm_a_im, ssm_log_dt, ssm_b_re, ssm_b_im, ssm_c_re, ssm_c_im, ssm_d, ssm_w_glu, ssm_b_glu, w_branch_attn, w_branch_ssm, w_out, w_up, conv_w, conv_b, w_down):
    depth = w_in.shape[0]
    xp, xs = x_prompt, x_sample
    acc_p, acc_s = [], []
    for layer in range(depth):
        wts = tuple(a[layer] for a in (
            norm_mix_pre, norm_mix_post, norm_ffn_pre, norm_ffn_post, w_in, sb_bias, ssm_a_re, ssm_a_im, ssm_log_dt,
            ssm_b_re, ssm_b_im, ssm_c_re, ssm_c_im, ssm_d, ssm_w_glu, ssm_b_glu, w_branch_attn, w_branch_ssm, w_out,
            w_up, conv_w, conv_b, w_down))
        out_p, out_s = _layer(xp, xs, cache_k[layer], cache_v[layer], page_table, state_ssm_re[layer],
                              state_ssm_im[layer], state_ffn_conv[layer], wts)
        xp, xs = out_p[0], out_s[0]
        acc_p.append(out_p[1:])
        acc_s.append(out_s[1:])
    stack = lambda acc, n: jnp.stack([a[n] for a in acc])
    kp, vp, hrp, hip, cp = (stack(acc_p, n) for n in range(5))
    ks, vs, hrs, his, cs = (stack(acc_s, n) for n in range(5))
    return (xp, xs, kp, vp, ks, vs, hrp, hip, hrs, his, cp, cs)
```

```python
import functools
import math

import jax
import jax.numpy as jnp
from jax import lax
from jax.experimental import pallas as pl
from jax.experimental.pallas import tpu as pltpu

RMS_EPS = 1e-6
LOG2E = math.log2(math.e)
HEAD_DIM = 64
HEADS_PER_BLOCK = 2
LANES = 128
SUBLANES = 8
SSM_BLOCK_CH = 128
VMEM_LIMIT = 56 * 1024 * 1024
ATTN_Q_TILE = 512
ATTN_K_TILE = 512
ATTN_CUMSUM_BLOCK = 256
FFN_COL_TILE = 512
FFN_COL_PART = 512
IN_PROJ_COL_TILE = 1024
BF16 = jnp.bfloat16
F32 = jnp.float32


def _cparams(sem):
    return pltpu.CompilerParams(dimension_semantics=sem, vmem_limit_bytes=VMEM_LIMIT)


def _pick_tile(n, target, quantum=SUBLANES):
    best = None
    for t in range(quantum, min(n, target) + 1, quantum):
        if n % t == 0:
            best = t
    assert best is not None, (n, target, quantum)
    return best


def _dot(a, b):
    return jnp.dot(a, b, preferred_element_type=F32)


def _dot_nt(a, b):
    return lax.dot_general(a, b, (((1,), (1,)), ((), ())), preferred_element_type=F32)


def _sigmoid(x):
    return 1.0 / (1.0 + jnp.exp(-x))


def _gelu(x):
    c = math.sqrt(2.0 / math.pi)
    return 0.5 * x * (1.0 + jnp.tanh(x * (c + (c * 0.044715) * (x * x))))


def _rms(x):
    return x * lax.rsqrt(jnp.mean(x * x, axis=-1, keepdims=True) + RMS_EPS)


def _softplus(z):
    return jnp.maximum(z, 0.0) + jnp.log(1.0 + jnp.exp2(jnp.abs(z) * (-LOG2E)))


def _ssm_discretise(ar, ai, ldt):
    dt = jnp.exp(ldt)
    mag = jnp.exp(dt * ar)
    return mag * jnp.cos(dt * ai), mag * jnp.sin(dt * ai)


def _ssm_prep_kernel(ar_ref, ai_ref, ldt_ref, ar3_ref, ai3_ref, ldt3_ref, br_ref, bi_ref,
                     powr_ref, powi_ref, bbr_ref, bbi_ref):
    ar, ai = ar3_ref[...], ai3_ref[...]
    abr, abi = _ssm_discretise(ar, ai, ldt3_ref[...])
    nr, ni = abr - 1.0, abi
    den = ar * ar + ai * ai
    fr, fi = (nr * ar + ni * ai) / den, (ni * ar - nr * ai) / den
    br, bi = br_ref[...], bi_ref[...]
    bbr_ref[...] = fr * br - fi * bi
    bbi_ref[...] = fr * bi + fi * br
    abr, abi = _ssm_discretise(ar_ref[...], ai_ref[...], ldt_ref[...])
    pr, pi = abr, abi
    powr_ref[0], powi_ref[0] = pr, pi
    for k in range(1, SUBLANES):
        pr, pi = pr * abr - pi * abi, pr * abi + pi * abr
        powr_ref[k], powi_ref[k] = pr, pi


def _ssm_prep(a_re, a_im, log_dt, b_re, b_im):
    g, p = a_re.shape
    c = b_re.shape[-1]
    out = (jax.ShapeDtypeStruct((SUBLANES, g, p), F32),) * 2 + (jax.ShapeDtypeStruct((g, c, p), F32),) * 2
    return pl.pallas_call(_ssm_prep_kernel, out_shape=out, name="ssm_prep")(
        a_re, a_im, log_dt.reshape(g, 1), a_re.reshape(g, 1, p), a_im.reshape(g, 1, p), log_dt.reshape(g, 1, 1),
        b_re.transpose(0, 2, 1), b_im.transpose(0, 2, 1))


def _ssm_block_weights(bbr, bbi, c_re, c_im):
    g, c, p = bbr.shape
    gb = SSM_BLOCK_CH // c
    nb = g // gb
    eye = jnp.eye(gb, dtype=F32)
    bb = jnp.stack([bbr, bbi], axis=0).reshape(2, nb, gb, c, p)
    bblk = jnp.einsum("rjgcp,gh->jgcrhp", bb, eye).reshape(nb, gb * c, 2 * gb * p)
    cc = jnp.stack([c_re, -c_im], axis=0).reshape(2, nb, gb, c, p)
    cblk = jnp.einsum("rjgcp,gh->jrgphc", cc, eye).reshape(nb, 2 * gb * p, gb * c)
    return bblk.astype(BF16), cblk.astype(BF16)


def _scan_consts(powr, powi, seg, gb):
    _, g, p = powr.shape
    nb = g // gb
    rr = jnp.arange(SUBLANES) % seg
    kinds = []
    for d in (1, 2, 4):
        m = (rr >= d).astype(F32)[:, None, None]
        kinds += [m * powr[d - 1][None], m * powi[d - 1][None]]
    kinds += [powr[rr], powi[rr]]
    k = jnp.stack(kinds, axis=0)
    return k.reshape(8, SUBLANES, nb, gb * p).transpose(2, 0, 1, 3)


def _inproj_kernel(x_ref, g_ref, w_ref, o_ref, xn_ref):
    @pl.when(pl.program_id(1) == 0)
    def _():
        xn_ref[...] = (_rms(x_ref[...]) * g_ref[...]).astype(BF16)

    o_ref[...] = _dot(xn_ref[...], w_ref[...])


def _in_proj(x, g, w_bf16, tm):
    r, d = x.shape
    n = w_bf16.shape[1]
    tn = _pick_tile(n, IN_PROJ_COL_TILE, LANES)
    return pl.pallas_call(
        _inproj_kernel, grid=(r // tm, n // tn),
        in_specs=[pl.BlockSpec((tm, d), lambda i, j: (i, 0)),
                  pl.BlockSpec((1, d), lambda i, j: (0, 0)),
                  pl.BlockSpec((d, tn), lambda i, j: (0, j))],
        out_specs=pl.BlockSpec((tm, tn), lambda i, j: (i, j)),
        out_shape=jax.ShapeDtypeStruct((r, n), F32),
        scratch_shapes=[pltpu.VMEM((tm, d), BF16)],
        compiler_params=_cparams(("arbitrary", "arbitrary")), name="in_proj")(x, g.reshape(1, d), w_bf16)


def _sb_weights(z, mask, tail, tri2, latest_first):
    nk = tri2.shape[1]
    n = z.shape[1] // nk
    sp = _softplus(z)
    if mask is not None:
        sp = jnp.where(mask, sp, 0.0)
    hi = sp.astype(BF16)
    lo = (sp - hi.astype(F32)).astype(BF16)
    parts = [None] * n
    for c in (range(n) if latest_first else reversed(range(n))):
        cols = slice(c * nk, (c + 1) * nk)
        local = _dot(jnp.concatenate([hi[:, cols], lo[:, cols]], axis=1), tri2)
        parts[c] = local + tail
        tail = tail + local[:, 0:1]
    rsum = parts[0] if n == 1 else jnp.concatenate(parts, axis=1)
    w = jnp.exp(z - rsum)
    if mask is not None:
        w = jnp.where(mask, w, 0.0)
    return w.astype(BF16), tail


def _tri2(n):
    tri = (jnp.arange(n)[:, None] >= jnp.arange(n)[None, :]).astype(BF16)
    return jnp.concatenate([tri, tri], axis=0)


def _attn_prompt_kernel(bias_ref, q_ref, k_ref, v_ref, tri_ref, o_ref, kb_ref, vb_ref, acc_ref, *, tq, tk):
    hp, qi = pl.program_id(1), pl.program_id(2)
    nd = tq // tk

    @pl.when(qi == 0)
    def _():
        kb_ref[...] = k_ref[...].astype(BF16)
        vb_ref[...] = v_ref[...].astype(BF16)

    lane = lax.broadcasted_iota(jnp.int32, (1, LANES), 1)
    first = lane < HEAD_DIM
    q = q_ref[...] * (HEAD_DIM ** -0.5)
    qh = (jnp.where(first, q, 0.0).astype(BF16), jnp.where(first, 0.0, q).astype(BF16))
    bias = (bias_ref[HEADS_PER_BLOCK * hp], bias_ref[HEADS_PER_BLOCK * hp + 1])
    tri2 = tri_ref[...]
    qpos = qi * tq + lax.broadcasted_iota(jnp.int32, (tq, tk), 0)
    kcol = lax.broadcasted_iota(jnp.int32, (tq, tk), 1)

    def block(j, h, tail, masked):
        start = pl.multiple_of(j * tk, tk)
        z = _dot_nt(qh[h], kb_ref[pl.ds(start, tk), :]) + bias[h]
        mask = (j * tk + kcol < qpos) if masked else None
        w, tail = _sb_weights(z, mask, tail, tri2, latest_first=False)
        return _dot(w, vb_ref[pl.ds(start, tk), :]), tail

    tails = [jnp.zeros((tq, 1), F32)] * HEADS_PER_BLOCK
    for dd in reversed(range(nd)):
        for h in range(HEADS_PER_BLOCK):
            pv, tails[h] = block(qi * nd + dd, h, tails[h], True)
            if dd == nd - 1:
                acc_ref[h] = pv
            else:
                acc_ref[h] += pv

    def body(it, tails):
        j = qi * nd - 1 - it
        new = []
        for h in range(HEADS_PER_BLOCK):
            pv, tl = block(j, h, tails[h], False)
            acc_ref[h] += pv
            new.append(tl)
        return tuple(new)

    lax.fori_loop(0, qi * nd, body, tuple(tails))
    o_ref[...] = jnp.where(first, acc_ref[0], acc_ref[1])


def _attn_prompt(proj, sb_bias, bsz, seq, width):
    r = proj.shape[0]
    tk = _pick_tile(seq, ATTN_K_TILE, LANES)
    tq = _pick_tile(seq, ATTN_Q_TILE, tk)
    tc = _pick_tile(tk, ATTN_CUMSUM_BLOCK, LANES)
    nq = seq // tq
    nhp = width // LANES
    return pl.pallas_call(
        functools.partial(_attn_prompt_kernel, tq=tq, tk=tk),
        grid=(bsz, nhp, nq),
        in_specs=[pl.BlockSpec(memory_space=pltpu.SMEM),
                  pl.BlockSpec((tq, LANES), lambda b, hp, qi: (b * nq + qi, hp)),
                  pl.BlockSpec((seq, LANES), lambda b, hp, qi: (b, nhp + hp)),
                  pl.BlockSpec((seq, LANES), lambda b, hp, qi: (b, 2 * nhp + hp)),
                  pl.BlockSpec((2 * tc, tc), lambda b, hp, qi: (0, 0))],
        out_specs=pl.BlockSpec((tq, LANES), lambda b, hp, qi: (b * nq + qi, hp)),
        out_shape=jax.ShapeDtypeStruct((r, width), F32),
        scratch_shapes=[pltpu.VMEM((seq, LANES), BF16), pltpu.VMEM((seq, LANES), BF16),
                        pltpu.VMEM((HEADS_PER_BLOCK, tq, LANES), F32)],
        compiler_params=_cparams(("arbitrary", "arbitrary", "arbitrary")), name="attn_prompt")(
            sb_bias, proj, proj, proj, _tri2(tc))


def _attn_sample_kernel(pt_ref, q_ref, kn_ref, vn_ref, bias_ref, tri_ref, *rest, n_heads, n_q, pages_per_step):
    del pt_ref
    k_refs = rest[:pages_per_step]
    v_refs = rest[pages_per_step:2 * pages_per_step]
    o_ref, wq_ref, acc_ref, tail_ref, pad_ref = rest[2 * pages_per_step:]
    s = pl.program_id(1)
    nrow = n_q * n_heads
    width = n_heads * HEAD_DIM
    assert n_heads & (n_heads - 1) == 0 and HEAD_DIM & (HEAD_DIM - 1) == 0
    head_shift, dim_shift = n_heads.bit_length() - 1, HEAD_DIM.bit_length() - 1
    r_head = lax.broadcasted_iota(jnp.int32, (nrow, width), 0) & (n_heads - 1)
    c_head = lax.broadcasted_iota(jnp.int32, (nrow, width), 1) >> dim_shift
    own_head = r_head == c_head
    page = tri_ref.shape[1]
    tri2 = tri_ref[...]
    bias = bias_ref[...]

    @pl.when(s == 0)
    def _():
        q = q_ref[0] * (HEAD_DIM ** -0.5)
        rows = jnp.concatenate([jnp.broadcast_to(q[i:i + 1], (n_heads, width)) for i in range(n_q)], axis=0)
        wq_ref[...] = jnp.where(own_head, rows, 0.0).astype(BF16)
        pad_ref[...] = jnp.zeros_like(pad_ref)
        pad_ref[0:SUBLANES, :] = kn_ref[0]
        z = _dot_nt(wq_ref[...], pad_ref[...].astype(BF16)) + bias
        pad_ref[0:SUBLANES, :] = vn_ref[0]
        kpos = lax.broadcasted_iota(jnp.int32, (nrow, page), 1)
        qpos = lax.broadcasted_iota(jnp.int32, (nrow, page), 0) >> head_shift
        w, tl = _sb_weights(z, kpos < qpos, jnp.zeros((nrow, 1), F32), tri2, latest_first=True)
        acc_ref[...] = _dot(w, pad_ref[...].astype(BF16))
        tail_ref[...] = tl

    kcat = jnp.concatenate([r[0].astype(BF16) for r in k_refs], axis=1)
    vcat = jnp.concatenate([r[0].astype(BF16) for r in v_refs], axis=1)
    z = _dot(wq_ref[...], kcat) + bias
    w, tail = _sb_weights(z, None, tail_ref[...], tri2, latest_first=True)
    acc = acc_ref[...] + _dot_nt(w, vcat)
    acc_ref[...] = acc
    tail_ref[...] = tail

    @pl.when(s == pl.num_programs(1) - 1)
    def _():
        own = jnp.where(own_head, acc, 0.0)
        o_ref[0] = jnp.concatenate(
            [jnp.sum(own[i * n_heads:(i + 1) * n_heads], axis=0, keepdims=True) for i in range(n_q)], axis=0)


def _attn_sample(q, k_new, v_new, sb_bias, cache_k, cache_v, page_table, bs, n_q):
    n_pool, page, n_heads, hd = cache_k.shape
    width = n_heads * hd
    n_pages = page_table.shape[1]
    pps = _pick_tile(n_pages, 8, 1)
    nrow = n_q * n_heads
    ck = cache_k.transpose(0, 2, 3, 1).reshape(n_pool, width, page)
    cv = cache_v.transpose(0, 2, 3, 1).reshape(n_pool, width, page)
    pad = lambda a: jnp.pad(a.reshape(bs, n_q, width), ((0, 0), (0, SUBLANES - n_q), (0, 0)))
    bias_col = jnp.tile(sb_bias, n_q).reshape(nrow, 1)
    tri = _tri2(page)

    def page_spec(p):
        return pl.BlockSpec((1, width, page),
                            lambda b, s, pt, p=p: (pt[b, n_pages - 1 - (s * pps + p)], 0, 0))

    grid_spec = pltpu.PrefetchScalarGridSpec(
        num_scalar_prefetch=1, grid=(bs, n_pages // pps),
        in_specs=[pl.BlockSpec((1, n_q, width), lambda b, s, pt: (b, 0, 0)),
                  pl.BlockSpec((1, SUBLANES, width), lambda b, s, pt: (b, 0, 0)),
                  pl.BlockSpec((1, SUBLANES, width), lambda b, s, pt: (b, 0, 0)),
                  pl.BlockSpec((nrow, 1), lambda b, s, pt: (0, 0)),
                  pl.BlockSpec((2 * page, page), lambda b, s, pt: (0, 0))]
                 + [page_spec(p) for p in range(pps)] * 2,
        out_specs=pl.BlockSpec((1, n_q, width), lambda b, s, pt: (b, 0, 0)),
        scratch_shapes=[pltpu.VMEM((nrow, width), BF16), pltpu.VMEM((nrow, width), F32),
                        pltpu.VMEM((nrow, 1), F32), pltpu.VMEM((page, width), F32)])
    out = pl.pallas_call(
        functools.partial(_attn_sample_kernel, n_heads=n_heads, n_q=n_q, pages_per_step=pps),
        grid_spec=grid_spec, out_shape=jax.ShapeDtypeStruct((bs, n_q, width), F32),
        compiler_params=_cparams(("arbitrary", "arbitrary")), name="attn_sample")(
            page_table, q.reshape(bs, n_q, width), pad(k_new), pad(v_new), bias_col, tri,
            *([ck] * pps), *([cv] * pps))
    return out.reshape(bs * n_q, width)


def _ssm_kernel(*refs, chunks_per_seq, per_row_h0):
    if per_row_h0:
        u_ref, bw_ref, cw_ref, k_ref, d_ref, h0_ref, gg_ref, hs_ref, bu_ref = refs
    else:
        u_ref, bw_ref, cw_ref, k_ref, d_ref, gg_ref, hs_ref, bu_ref, carry_ref = refs
    i, j = pl.program_id(0), pl.program_id(1)
    rows = u_ref.shape[0]
    half = bu_ref.shape[1] // 2
    u = u_ref[...]
    bu_ref[...] = _dot(u.astype(BF16), bw_ref[0])

    if not per_row_h0:
        @pl.when(i % chunks_per_seq == 0)
        def _():
            carry_ref[j] = jnp.zeros((1, 2 * half), F32)

    def tile(tix, carry):
        r0 = pl.multiple_of(tix * SUBLANES, SUBLANES)
        br = bu_ref[pl.ds(r0, SUBLANES), 0:half]
        bi = bu_ref[pl.ds(r0, SUBLANES), half:2 * half]
        for n, d in enumerate((1, 2, 4)):
            ar, ai = k_ref[0, 2 * n], k_ref[0, 2 * n + 1]
            sr, si = pltpu.roll(br, d, 0), pltpu.roll(bi, d, 0)
            br, bi = br + ar * sr - ai * si, bi + ar * si + ai * sr
        pr, pi = k_ref[0, 6], k_ref[0, 7]
        if per_row_h0:
            cr = h0_ref[pl.ds(r0, SUBLANES), 0:half]
            ci = h0_ref[pl.ds(r0, SUBLANES), half:2 * half]
        else:
            cr, ci = carry
        hr = br + pr * cr - pi * ci
        hi = bi + pr * ci + pi * cr
        bu_ref[pl.ds(r0, SUBLANES), 0:half] = hr
        bu_ref[pl.ds(r0, SUBLANES), half:2 * half] = hi
        return (hr[SUBLANES - 1:SUBLANES], hi[SUBLANES - 1:SUBLANES])

    if per_row_h0:
        init = (jnp.zeros((1, half), F32),) * 2
    else:
        init = (carry_ref[j, :, 0:half], carry_ref[j, :, half:2 * half])
    last = lax.fori_loop(0, rows // SUBLANES, tile, init)
    if per_row_h0:
        hs_ref[...] = bu_ref[...]
    else:
        carry_ref[j, :, 0:half] = last[0]
        carry_ref[j, :, half:2 * half] = last[1]
        hs_ref[0, 0] = bu_ref[rows - SUBLANES:rows, :]
    y = _dot(bu_ref[...].astype(BF16), cw_ref[0]) + d_ref[...] * u
    gg_ref[...] = _gelu(y)


def _ssm(proj, u_col, bblk, cblk, consts, d_skip, seq, tc, h0_rows=None):
    r = proj.shape[0]
    width = d_skip.shape[0]
    nb, cb, sw = bblk.shape
    per_row = h0_rows is not None
    cps = max(seq // tc, 1)
    u_blk = u_col // cb
    assert u_blk * cb == u_col
    in_specs = [pl.BlockSpec((tc, cb), lambda i, j: (i, u_blk + j)),
                pl.BlockSpec((1, cb, sw), lambda i, j: (j, 0, 0)),
                pl.BlockSpec((1, sw, cb), lambda i, j: (j, 0, 0)),
                pl.BlockSpec((1, 8, SUBLANES, sw // 2), lambda i, j: (j, 0, 0, 0)),
                pl.BlockSpec((1, cb), lambda i, j: (0, j))]
    args = [proj, bblk, cblk, consts, d_skip.reshape(1, width)]
    scratch = [pltpu.VMEM((tc, sw), F32)]
    if per_row:
        assert r == tc
        in_specs.append(pl.BlockSpec((tc, sw), lambda i, j: (0, j)))
        args.append(h0_rows)
        hs_shape = jax.ShapeDtypeStruct((r, nb * sw), F32)
        hs_spec = pl.BlockSpec((tc, sw), lambda i, j: (0, j))
    else:
        hs_shape = jax.ShapeDtypeStruct((r // tc, nb, SUBLANES, sw), F32)
        hs_spec = pl.BlockSpec((1, 1, SUBLANES, sw), lambda i, j: (i, j, 0, 0))
        scratch.append(pltpu.VMEM((nb, 1, sw), F32))
    return pl.pallas_call(
        functools.partial(_ssm_kernel, chunks_per_seq=cps, per_row_h0=per_row),
        grid=(r // tc, nb), in_specs=in_specs,
        out_specs=[pl.BlockSpec((tc, cb), lambda i, j: (i, j)), hs_spec],
        out_shape=[jax.ShapeDtypeStruct((r, width), F32), hs_shape],
        scratch_shapes=scratch,
        compiler_params=_cparams(("arbitrary", "arbitrary")), name="ssm")(*args)


def _mix_kernel(oa_ref, gg_ref, ga_ref, gs_ref, x_ref, wglu_ref, bglu_ref, wba_ref, wbs_ref, wout_ref, g_ref, x1_ref):
    gg = gg_ref[...]
    o_ssm = gg * _sigmoid(_dot(gg.astype(BF16), wglu_ref[...]) + bglu_ref[...])
    merged = (_sigmoid(ga_ref[...]) * _dot(oa_ref[...].astype(BF16), wba_ref[...])
              + _sigmoid(gs_ref[...]) * _dot(o_ssm.astype(BF16), wbs_ref[...]))
    mo = _dot(merged.astype(BF16), wout_ref[...])
    x1_ref[...] = x_ref[...] + _rms(mo) * g_ref[...]


def _mix(o_attn, gg, proj, gate_col, x, wglu, bglu, wba, wbs, wout, g, tm):
    r, d = x.shape
    w = o_attn.shape[1]
    gblk = gate_col // d
    assert gblk * d == gate_col
    row = lambda n, c=0: pl.BlockSpec((tm, n), lambda i: (i, c))
    const = lambda a, b: pl.BlockSpec((a, b), lambda i: (0, 0), pipeline_mode=pl.Buffered(1))
    return pl.pallas_call(
        _mix_kernel, grid=(r // tm,),
        in_specs=[row(w), row(w), row(d, gblk), row(d, gblk + 1), row(d), const(w, w), const(1, w), const(w, d),
                  const(w, d), const(d, d), const(1, d)],
        out_specs=row(d), out_shape=jax.ShapeDtypeStruct((r, d), F32),
        compiler_params=_cparams(("arbitrary",)), name="mix")(
            o_attn, gg, proj, proj, x, wglu, bglu.reshape(1, w), wba, wbs, wout, g.reshape(1, d))


def _ffn_kernel(*refs, seq, has_prev, conv_w):
    if has_prev:
        (x_ref, g3_ref, wg_ref, wv_ref, cw_ref, cb_ref, wd_ref, g4_ref, p1_ref, p2_ref,
         y_ref, cs_ref, xn_ref, acc_ref, gbuf_ref, carry_ref) = refs
    else:
        (x_ref, g3_ref, wg_ref, wv_ref, cw_ref, cb_ref, wd_ref, g4_ref,
         y_ref, cs_ref, xn_ref, acc_ref, gbuf_ref, carry_ref) = refs
    i, j = pl.program_id(0), pl.program_id(1)
    tm = x_ref.shape[0]
    assert conv_w == 3

    @pl.when(j == 0)
    def _():
        xn_ref[...] = (_rms(x_ref[...]) * g3_ref[...]).astype(BF16)
        acc_ref[...] = jnp.zeros_like(acc_ref)

    @pl.when(i == 0)
    def _():
        carry_ref[j] = jnp.zeros(carry_ref.shape[1:], F32)

    xn = xn_ref[...]
    t = (i * tm + lax.broadcasted_iota(jnp.int32, (tm, 1), 0)) % seq
    keep = cs_ref.shape[1]
    tn = wg_ref.shape[1]
    part = min(tn, FFN_COL_PART)
    for c0 in range(0, tn, part):
        cols = slice(c0, c0 + part)
        gate = _dot(xn, wg_ref[:, cols])
        val = _dot(xn, wv_ref[:, cols])
        gbuf_ref[0:SUBLANES, cols] = carry_ref[j, :, cols]
        gbuf_ref[SUBLANES:SUBLANES + tm, cols] = gate
        g1 = gbuf_ref[pl.ds(SUBLANES - 1, tm), cols]
        g2 = gbuf_ref[pl.ds(SUBLANES - 2, tm), cols]
        g1 = jnp.where(t >= 1, g1, p1_ref[:, cols] if has_prev else 0.0)
        g2 = jnp.where(t >= 2, g2, p2_ref[:, cols] if has_prev else 0.0)
        cw = cw_ref[:, cols]
        conv = cb_ref[:, cols] + cw[0:1] * g2 + cw[1:2] * g1 + cw[2:3] * gate
        h = _gelu(conv) * val
        acc_ref[...] += _dot(h.astype(BF16), wd_ref[cols, :])
        carry_ref[j, :, cols] = gate[tm - SUBLANES:tm]
        cs_ref[0, :, cols] = gate[tm - keep:tm]

    @pl.when(j == pl.num_programs(1) - 1)
    def _():
        y_ref[...] = x_ref[...] + _rms(acc_ref[...]) * g4_ref[...]


def _ffn(x1, g3, wg, wv, cw, cb, wd, g4, seq, tm, prev=None):
    r, d = x1.shape
    dffp = wg.shape[1]
    tn = FFN_COL_TILE
    nj = dffp // tn
    has_prev = prev is not None
    keep = tm if has_prev else SUBLANES
    in_specs = [pl.BlockSpec((tm, d), lambda i, j: (i, 0)),
                pl.BlockSpec((1, d), lambda i, j: (0, 0)),
                pl.BlockSpec((d, tn), lambda i, j: (0, j)),
                pl.BlockSpec((d, tn), lambda i, j: (0, j)),
                pl.BlockSpec((3, tn), lambda i, j: (0, j)),
                pl.BlockSpec((1, tn), lambda i, j: (0, j)),
                pl.BlockSpec((tn, d), lambda i, j: (j, 0)),
                pl.BlockSpec((1, d), lambda i, j: (0, 0))]
    args = [x1, g3.reshape(1, d), wg, wv, cw, cb.reshape(1, dffp), wd, g4.reshape(1, d)]
    if has_prev:
        assert r == tm
        in_specs += [pl.BlockSpec((tm, tn), lambda i, j: (0, j))] * 2
        args += list(prev)
    n_cs = r // tm
    return pl.pallas_call(
        functools.partial(_ffn_kernel, seq=seq, has_prev=has_prev, conv_w=cw.shape[0]),
        grid=(r // tm, nj), in_specs=in_specs,
        out_specs=[pl.BlockSpec((tm, d), lambda i, j: (i, 0)),
                   pl.BlockSpec((1, keep, tn), lambda i, j: (i, 0, j))],
        out_shape=[jax.ShapeDtypeStruct((r, d), F32), jax.ShapeDtypeStruct((n_cs, keep, dffp), F32)],
        scratch_shapes=[pltpu.VMEM((tm, d), BF16), pltpu.VMEM((tm, d), F32),
                        pltpu.VMEM((tm + SUBLANES, tn), F32), pltpu.VMEM((nj, SUBLANES, tn), F32)],
        compiler_params=_cparams(("arbitrary", "arbitrary")), name="ffn")(*args)


def _layer(xp, xs, cache_k, cache_v, page_table, h0_re, h0_im, conv_prev, wts):
    (norm_mix_pre, norm_mix_post, norm_ffn_pre, norm_ffn_post, w_in, sb_bias, a_re, a_im, log_dt, b_re, b_im,
     c_re, c_im, d_skip, w_glu, b_glu, w_ba, w_bs, w_out, w_up, conv_w, conv_b, w_down) = wts
    bp, seq, d = xp.shape
    bs, nq, _ = xs.shape
    n_heads = sb_bias.shape[0]
    aw = n_heads * HEAD_DIM
    g, p = a_re.shape
    c = b_re.shape[-1]
    sw_total = g * c
    gb = SSM_BLOCK_CH // c
    nb = g // gb
    dff = conv_w.shape[1]
    dffp = -(-dff // FFN_COL_TILE) * FFN_COL_TILE
    u_col, gate_col = 3 * aw, 3 * aw + sw_total
    assert w_in.shape[1] == gate_col + 2 * d

    w_in_b, w_glu_b, w_ba_b, w_bs_b, w_out_b = (a.astype(BF16) for a in (w_in, w_glu, w_ba, w_bs, w_out))
    padc = lambda a: jnp.pad(a, ((0, 0), (0, dffp - dff)))
    wg_b = padc(w_up[:, :dff]).astype(BF16)
    wv_b = padc(w_up[:, dff:]).astype(BF16)
    wd_b = jnp.pad(w_down, ((0, dffp - dff), (0, 0))).astype(BF16)
    cw_p, cb_p = padc(conv_w), jnp.pad(conv_b, (0, dffp - dff))

    powr, powi, bbr, bbi = _ssm_prep(a_re, a_im, log_dt, b_re, b_im)
    bblk, cblk = _ssm_block_weights(bbr, bbi, c_re, c_im)

    def states(h):
        h = h.reshape(-1, nb, 2, gb, p)
        return h[:, :, 0].reshape(-1, g, p), h[:, :, 1].reshape(-1, g, p)

    rp = bp * seq
    x2 = xp.reshape(rp, d)
    proj = _in_proj(x2, norm_mix_pre, w_in_b, _pick_tile(rp, 512))
    k, v = proj[:, aw:2 * aw], proj[:, 2 * aw:3 * aw]
    o_attn = _attn_prompt(proj, sb_bias, bp, seq, aw)
    tc = _pick_tile(seq, 512)
    gg, hs = _ssm(proj, u_col, bblk, cblk, _scan_consts(powr, powi, SUBLANES, gb), d_skip, seq, tc)
    hrp, hip = states(hs.reshape(bp, seq // tc, nb, SUBLANES, -1)[:, -1, :, SUBLANES - 1, :].reshape(bp, -1))
    x1 = _mix(o_attn, gg, proj, gate_col, x2, w_glu_b, b_glu, w_ba_b, w_bs_b, w_out_b, norm_mix_post,
              _pick_tile(rp, 256))
    tmf = _pick_tile(seq, 512)
    yp, csp = _ffn(x1, norm_ffn_pre, wg_b, wv_b, cw_p, cb_p, wd_b, norm_ffn_post, seq, tmf)
    out_p = (yp.reshape(bp, seq, d), k.reshape(bp, seq, n_heads, HEAD_DIM), v.reshape(bp, seq, n_heads, HEAD_DIM),
             hrp, hip, csp.reshape(bp, seq // tmf, SUBLANES, dffp)[:, -1, SUBLANES - (conv_w.shape[0] - 1):, :dff])

    rs = bs * nq
    assert SUBLANES % nq == 0
    xs2 = xs.reshape(rs, d)
    proj = _in_proj(xs2, norm_mix_pre, w_in_b, rs)
    q, k, v = proj[:, :aw], proj[:, aw:2 * aw], proj[:, 2 * aw:3 * aw]
    o_attn = _attn_sample(q, k, v, sb_bias, cache_k, cache_v, page_table, bs, nq)
    h0 = jnp.stack([h0_re.reshape(bs, nb, gb * p), h0_im.reshape(bs, nb, gb * p)], axis=2).reshape(bs, nb * 2 * gb * p)
    gg, hs = _ssm(proj, u_col, bblk, cblk, _scan_consts(powr, powi, nq, gb), d_skip, nq, rs,
                  h0_rows=jnp.repeat(h0, nq, axis=0))
    hrs, his = states(hs.reshape(bs, nq, -1)[:, nq - 1])
    x1 = _mix(o_attn, gg, proj, gate_col, xs2, w_glu_b, b_glu, w_ba_b, w_bs_b, w_out_b, norm_mix_post, rs)
    t_row = jnp.arange(rs) % nq
    prev = padc(conv_prev.reshape(bs * (conv_w.shape[0] - 1), dff)).reshape(bs, conv_w.shape[0] - 1, dffp)
    prev_rows = jnp.repeat(prev, nq, axis=0)
    p1 = jnp.where((t_row == 0)[:, None], prev_rows[:, 1], 0.0)
    p2 = jnp.where((t_row == 0)[:, None], prev_rows[:, 0], jnp.where((t_row == 1)[:, None], prev_rows[:, 1], 0.0))
    ys, css = _ffn(x1, norm_ffn_pre, wg_b, wv_b, cw_p, cb_p, wd_b, norm_ffn_post, nq, rs, prev=(p1, p2))
    css = css.reshape(bs, nq, dffp)[:, nq - (conv_w.shape[0] - 1):, :dff]
    out_s = (ys.reshape(bs, nq, d), k.reshape(bs, nq, n_heads, HEAD_DIM), v.reshape(bs, nq, n_heads, HEAD_DIM),
             hrs, his, css)
    return out_p, out_s


def kernel(x_prompt, x_sample, cache_k, cache_v, page_table, state_ssm_re, state_ssm_im, state_ffn_conv, norm_mix_pre, norm_mix_post, norm_ffn_pre, norm_ffn_post, w_in, sb_bias, ssm_a_re, ssm_a_im, ssm_log_dt, ssm_b_re, ssm_b_im, ssm_c_re, ssm_c_im, ssm_d, ssm_w_glu, ssm_b_glu, w_branch_attn, w_branch_ssm, w_out, w_up, conv_w, conv_b, w_down):
    depth = w_in.shape[0]
    xp, xs = x_prompt, x_sample
    acc_p, acc_s = [], []
    for layer in range(depth):
        wts = tuple(a[layer] for a in (
            norm_mix_pre, norm_mix_post, norm_ffn_pre, norm_ffn_post, w_in, sb_bias, ssm_a_re, ssm_a_im, ssm_log_dt,
            ssm_b_re, ssm_b_im, ssm_c_re, ssm_c_im, ssm_d, ssm_w_glu, ssm_b_glu, w_branch_attn, w_branch_ssm, w_out,
            w_up, conv_w, conv_b, w_down))
        out_p, out_s = _layer(xp, xs, cache_k[layer], cache_v[layer], page_table, state_ssm_re[layer],
                              state_ssm_im[layer], state_ffn_conv[layer], wts)
        xp, xs = out_p[0], out_s[0]
        acc_p.append(out_p[1:])
        acc_s.append(out_s[1:])
    stack = lambda acc, n: jnp.stack([a[n] for a in acc])
    kp, vp, hrp, hip, cp = (stack(acc_p, n) for n in range(5))
    ks, vs, hrs, his, cs = (stack(acc_s, n) for n in range(5))
    return (xp, xs, kp, vp, ks, vs, hrp, hip, hrs, his, cp, cs)
```

```python
import functools
import math

import jax
import jax.numpy as jnp
from jax import lax
from jax.experimental import pallas as pl
from jax.experimental.pallas import tpu as pltpu

RMS_EPS = 1e-6
LOG2E = math.log2(math.e)
HEAD_DIM = 64
HEADS_PER_BLOCK = 2
LANES = 128
SUBLANES = 8
SSM_BLOCK_CH = 128
VMEM_LIMIT = 56 * 1024 * 1024
ATTN_Q_TILE = 512
ATTN_K_TILE = 512
ATTN_CUMSUM_BLOCK = 256
SAMPLE_PAGE_GROUP = 8
FFN_COL_TILE = 512
FFN_COL_PART = 512
IN_PROJ_COL_TILE = 1024
IN_PROJ_ROW_TILE = 1024
BF16 = jnp.bfloat16
F32 = jnp.float32


def _cparams(sem):
    return pltpu.CompilerParams(dimension_semantics=sem, vmem_limit_bytes=VMEM_LIMIT)


def _pick_tile(n, target, quantum=SUBLANES):
    best = None
    for t in range(quantum, min(n, target) + 1, quantum):
        if n % t == 0:
            best = t
    assert best is not None, (n, target, quantum)
    return best


def _dot(a, b):
    return jnp.dot(a, b, preferred_element_type=F32)


def _dot_nt(a, b):
    return lax.dot_general(a, b, (((1,), (1,)), ((), ())), preferred_element_type=F32)


def _sigmoid(x):
    return 1.0 / (1.0 + jnp.exp(-x))


def _gelu(x):
    c = math.sqrt(2.0 / math.pi)
    return 0.5 * x * (1.0 + jnp.tanh(x * (c + (c * 0.044715) * (x * x))))


def _rms(x):
    return x * lax.rsqrt(jnp.mean(x * x, axis=-1, keepdims=True) + RMS_EPS)


def _softplus(z):
    return jnp.maximum(z, 0.0) + jnp.log(1.0 + jnp.exp2(jnp.abs(z) * (-LOG2E)))


def _ssm_discretise(ar, ai, ldt):
    dt = jnp.exp(ldt)
    mag = jnp.exp(dt * ar)
    return mag * jnp.cos(dt * ai), mag * jnp.sin(dt * ai)


def _ssm_prep_kernel(ar_ref, ai_ref, ldt_ref, ar3_ref, ai3_ref, ldt3_ref, br_ref, bi_ref,
                     powr_ref, powi_ref, bbr_ref, bbi_ref):
    ar, ai = ar3_ref[...], ai3_ref[...]
    abr, abi = _ssm_discretise(ar, ai, ldt3_ref[...])
    nr, ni = abr - 1.0, abi
    den = ar * ar + ai * ai
    fr, fi = (nr * ar + ni * ai) / den, (ni * ar - nr * ai) / den
    br, bi = br_ref[...], bi_ref[...]
    bbr_ref[...] = fr * br - fi * bi
    bbi_ref[...] = fr * bi + fi * br
    abr, abi = _ssm_discretise(ar_ref[...], ai_ref[...], ldt_ref[...])
    pr, pi = abr, abi
    powr_ref[0], powi_ref[0] = pr, pi
    for k in range(1, SUBLANES):
        pr, pi = pr * abr - pi * abi, pr * abi + pi * abr
        powr_ref[k], powi_ref[k] = pr, pi


def _ssm_prep(a_re, a_im, log_dt, b_re, b_im):
    g, p = a_re.shape
    c = b_re.shape[-1]
    out = (jax.ShapeDtypeStruct((SUBLANES, g, p), F32),) * 2 + (jax.ShapeDtypeStruct((g, c, p), F32),) * 2
    return pl.pallas_call(_ssm_prep_kernel, out_shape=out, name="ssm_prep")(
        a_re, a_im, log_dt.reshape(g, 1), a_re.reshape(g, 1, p), a_im.reshape(g, 1, p), log_dt.reshape(g, 1, 1),
        b_re.transpose(0, 2, 1), b_im.transpose(0, 2, 1))


def _ssm_block_weights(bbr, bbi, c_re, c_im):
    g, c, p = bbr.shape
    gb = SSM_BLOCK_CH // c
    nb = g // gb
    eye = jnp.eye(gb, dtype=F32)
    bb = jnp.stack([bbr, bbi], axis=0).reshape(2, nb, gb, c, p)
    bblk = jnp.einsum("rjgcp,gh->jgcrhp", bb, eye).reshape(nb, gb * c, 2 * gb * p)
    cc = jnp.stack([c_re, -c_im], axis=0).reshape(2, nb, gb, c, p)
    cblk = jnp.einsum("rjgcp,gh->jrgphc", cc, eye).reshape(nb, 2 * gb * p, gb * c)
    return bblk.astype(BF16), cblk.astype(BF16)


def _scan_consts(powr, powi, seg, gb):
    _, g, p = powr.shape
    nb = g // gb
    rr = jnp.arange(SUBLANES) % seg
    kinds = []
    for d in (1, 2, 4):
        m = (rr >= d).astype(F32)[:, None, None]
        kinds += [m * powr[d - 1][None], m * powi[d - 1][None]]
    kinds += [powr[rr], powi[rr]]
    k = jnp.stack(kinds, axis=0)
    return k.reshape(8, SUBLANES, nb, gb * p).transpose(2, 0, 1, 3)


def _inproj_kernel(x_ref, g_ref, w_ref, o_ref, xn_ref):
    @pl.when(pl.program_id(1) == 0)
    def _():
        xn_ref[...] = (_rms(x_ref[...]) * g_ref[...]).astype(BF16)

    o_ref[...] = _dot(xn_ref[...], w_ref[...])


def _in_proj(x, g, w_bf16, tm):
    r, d = x.shape
    n = w_bf16.shape[1]
    tn = _pick_tile(n, IN_PROJ_COL_TILE, LANES)
    return pl.pallas_call(
        _inproj_kernel, grid=(r // tm, n // tn),
        in_specs=[pl.BlockSpec((tm, d), lambda i, j: (i, 0)),
                  pl.BlockSpec((1, d), lambda i, j: (0, 0)),
                  pl.BlockSpec((d, tn), lambda i, j: (0, j))],
        out_specs=pl.BlockSpec((tm, tn), lambda i, j: (i, j)),
        out_shape=jax.ShapeDtypeStruct((r, n), F32),
        scratch_shapes=[pltpu.VMEM((tm, d), BF16)],
        compiler_params=_cparams(("arbitrary", "arbitrary")), name="in_proj")(x, g.reshape(1, d), w_bf16)


def _sb_weights(z, mask, tail, tri2, latest_first):
    nk = tri2.shape[1]
    n = z.shape[1] // nk
    sp = _softplus(z)
    if mask is not None:
        sp = jnp.where(mask, sp, 0.0)
    hi = sp.astype(BF16)
    lo = (sp - hi.astype(F32)).astype(BF16)
    parts = [None] * n
    for c in (range(n) if latest_first else reversed(range(n))):
        cols = slice(c * nk, (c + 1) * nk)
        local = _dot(jnp.concatenate([hi[:, cols], lo[:, cols]], axis=1), tri2)
        parts[c] = local + tail
        tail = tail + local[:, 0:1]
    rsum = parts[0] if n == 1 else jnp.concatenate(parts, axis=1)
    w = jnp.exp(z - rsum)
    if mask is not None:
        w = jnp.where(mask, w, 0.0)
    return w.astype(BF16), tail


def _tri2(n):
    tri = (jnp.arange(n)[:, None] >= jnp.arange(n)[None, :]).astype(BF16)
    return jnp.concatenate([tri, tri], axis=0)


def _prompt_attn_step(hp, qi, half, bias_ref, q_ref, k_ref, v_ref, tri_ref, o_ref, kb_ref, vb_ref, acc_ref, tail_ref,
                      *, tq, tk):
    assert tq % tk == 0 or tk % tq == 0
    nd = max(tq // tk, 1)
    first_diag = (qi * tq) // tk

    lane = lax.broadcasted_iota(jnp.int32, (1, LANES), 1)
    first = lane < HEAD_DIM
    q = q_ref[...] * (HEAD_DIM ** -0.5)
    qh = (jnp.where(first, q, 0.0).astype(BF16), jnp.where(first, 0.0, q).astype(BF16))
    bias = (bias_ref[HEADS_PER_BLOCK * hp], bias_ref[HEADS_PER_BLOCK * hp + 1])
    tri2 = tri_ref[...]
    qpos = qi * tq + lax.broadcasted_iota(jnp.int32, (tq, tk), 0)
    kcol = lax.broadcasted_iota(jnp.int32, (tq, tk), 1)

    def block(j, h, tail, masked):
        start = pl.multiple_of(j * tk, tk)
        z = _dot_nt(qh[h], kb_ref[pl.ds(start, tk), :]) + bias[h]
        mask = (j * tk + kcol < qpos) if masked else None
        w, tail = _sb_weights(z, mask, tail, tri2, latest_first=False)
        return _dot(w, vb_ref[pl.ds(start, tk), :]), tail

    def body(it, tails):
        j = first_diag - 1 - it
        new = []
        for h in range(HEADS_PER_BLOCK):
            pv, tl = block(j, h, tails[h], False)
            acc_ref[h] += pv
            new.append(tl)
        return tuple(new)

    def start(n_off):
        @pl.when(qi == 0)
        def _():
            kb_ref[...] = k_ref[...].astype(BF16)
            vb_ref[...] = v_ref[...].astype(BF16)

        tails = [jnp.zeros((tq, 1), F32)] * HEADS_PER_BLOCK
        for dd in reversed(range(nd)):
            for h in range(HEADS_PER_BLOCK):
                pv, tails[h] = block(first_diag + dd, h, tails[h], True)
                if dd == nd - 1:
                    acc_ref[h] = pv
                else:
                    acc_ref[h] += pv
        return lax.fori_loop(0, n_off, body, tuple(tails))

    def finish(lo, tails):
        lax.fori_loop(lo, first_diag, body, tails)
        o_ref[...] = jnp.where(first, acc_ref[0], acc_ref[1])

    if half is None:
        finish(first_diag, start(first_diag))
    else:
        split = first_diag // 2

        @pl.when(half == 0)
        def _():
            tails = start(split)
            for h in range(HEADS_PER_BLOCK):
                tail_ref[h] = tails[h]

        @pl.when(half == 1)
        def _():
            finish(split, tuple(tail_ref[h] for h in range(HEADS_PER_BLOCK)))


def _attn_prompt_kernel(bias_ref, q_ref, k_ref, v_ref, tri_ref, o_ref, kb_ref, vb_ref, acc_ref, *, tq, tk):
    _prompt_attn_step(pl.program_id(1), pl.program_id(2), None, bias_ref, q_ref, k_ref, v_ref, tri_ref, o_ref,
                      kb_ref, vb_ref, acc_ref, None, tq=tq, tk=tk)


def _attn_prompt(proj, sb_bias, bsz, seq, width):
    r = proj.shape[0]
    tk = _pick_tile(seq, ATTN_K_TILE, LANES)
    tq = _pick_tile(seq, ATTN_Q_TILE, tk)
    tc = _pick_tile(tk, ATTN_CUMSUM_BLOCK, LANES)
    nq = seq // tq
    nhp = width // LANES
    return pl.pallas_call(
        functools.partial(_attn_prompt_kernel, tq=tq, tk=tk),
        grid=(bsz, nhp, nq),
        in_specs=[pl.BlockSpec(memory_space=pltpu.SMEM),
                  pl.BlockSpec((tq, LANES), lambda b, hp, qi: (b * nq + qi, hp)),
                  pl.BlockSpec((seq, LANES), lambda b, hp, qi: (b, nhp + hp)),
                  pl.BlockSpec((seq, LANES), lambda b, hp, qi: (b, 2 * nhp + hp)),
                  pl.BlockSpec((2 * tc, tc), lambda b, hp, qi: (0, 0))],
        out_specs=pl.BlockSpec((tq, LANES), lambda b, hp, qi: (b * nq + qi, hp)),
        out_shape=jax.ShapeDtypeStruct((r, width), F32),
        scratch_shapes=[pltpu.VMEM((seq, LANES), BF16), pltpu.VMEM((seq, LANES), BF16),
                        pltpu.VMEM((HEADS_PER_BLOCK, tq, LANES), F32)],
        compiler_params=_cparams(("arbitrary", "arbitrary", "arbitrary")), name="attn_prompt")(
            sb_bias, proj, proj, proj, _tri2(tc))


def _sample_attn_step(is_first, is_last, q_ref, kn_ref, vn_ref, bias_ref, tri_ref, k_refs, v_refs,
                      o_ref, wq_ref, acc_ref, tail_ref, pad_ref, *, n_heads, n_q):
    nrow = n_q * n_heads
    width = n_heads * HEAD_DIM
    assert n_heads & (n_heads - 1) == 0 and HEAD_DIM & (HEAD_DIM - 1) == 0
    head_shift, dim_shift = n_heads.bit_length() - 1, HEAD_DIM.bit_length() - 1
    r_head = lax.broadcasted_iota(jnp.int32, (nrow, width), 0) & (n_heads - 1)
    c_head = lax.broadcasted_iota(jnp.int32, (nrow, width), 1) >> dim_shift
    own_head = r_head == c_head
    page = tri_ref.shape[1]
    tri2 = tri_ref[...]
    bias = bias_ref[...]

    @pl.when(is_first)
    def _():
        q = q_ref[0] * (HEAD_DIM ** -0.5)
        rows = jnp.concatenate([jnp.broadcast_to(q[i:i + 1], (n_heads, width)) for i in range(n_q)], axis=0)
        wq_ref[...] = jnp.where(own_head, rows, 0.0).astype(BF16)
        pad_ref[...] = jnp.zeros_like(pad_ref)
        pad_ref[0:SUBLANES, :] = kn_ref[0]
        z = _dot_nt(wq_ref[...], pad_ref[...].astype(BF16)) + bias
        pad_ref[0:SUBLANES, :] = vn_ref[0]
        kpos = lax.broadcasted_iota(jnp.int32, (nrow, page), 1)
        qpos = lax.broadcasted_iota(jnp.int32, (nrow, page), 0) >> head_shift
        w, tl = _sb_weights(z, kpos < qpos, jnp.zeros((nrow, 1), F32), tri2, latest_first=True)
        acc_ref[...] = _dot(w, pad_ref[...].astype(BF16))
        tail_ref[...] = tl

    tail = tail_ref[...]
    acc = acc_ref[...]
    for g0 in range(0, len(k_refs), SAMPLE_PAGE_GROUP):
        kcat = jnp.concatenate([r[0].astype(BF16) for r in k_refs[g0:g0 + SAMPLE_PAGE_GROUP]], axis=1)
        vcat = jnp.concatenate([r[0].astype(BF16) for r in v_refs[g0:g0 + SAMPLE_PAGE_GROUP]], axis=1)
        z = _dot(wq_ref[...], kcat) + bias
        w, tail = _sb_weights(z, None, tail, tri2, latest_first=True)
        acc = acc + _dot_nt(w, vcat)
    acc_ref[...] = acc
    tail_ref[...] = tail

    @pl.when(is_last)
    def _():
        own = jnp.where(own_head, acc, 0.0)
        o_ref[0] = jnp.concatenate(
            [jnp.sum(own[i * n_heads:(i + 1) * n_heads], axis=0, keepdims=True) for i in range(n_q)], axis=0)


def _attn_sample_kernel(pt_ref, q_ref, kn_ref, vn_ref, bias_ref, tri_ref, *rest, n_heads, n_q, pages_per_step):
    del pt_ref
    k_refs = rest[:pages_per_step]
    v_refs = rest[pages_per_step:2 * pages_per_step]
    s = pl.program_id(1)
    _sample_attn_step(s == 0, s == pl.num_programs(1) - 1, q_ref, kn_ref, vn_ref, bias_ref, tri_ref, k_refs, v_refs,
                      *rest[2 * pages_per_step:], n_heads=n_heads, n_q=n_q)


def _sample_attn_operands(q, k_new, v_new, sb_bias, cache_k, cache_v, bs, n_q, pps, seq_and_group):
    n_pool, page, n_heads, hd = cache_k.shape
    width = n_heads * hd
    nrow = n_q * n_heads
    ck = cache_k.transpose(0, 2, 3, 1).reshape(n_pool, width, page)
    cv = cache_v.transpose(0, 2, 3, 1).reshape(n_pool, width, page)
    pad = lambda a: jnp.pad(a.reshape(bs, n_q, width), ((0, 0), (0, SUBLANES - n_q), (0, 0)))
    bias_col = jnp.tile(sb_bias, n_q).reshape(nrow, 1)

    def per_seq(rows):
        return pl.BlockSpec((1, rows, width), lambda *ids: (seq_and_group(*ids[:-1])[0], 0, 0))

    def page_spec(p):
        def index(*ids):
            pt = ids[-1]
            sb, grp = seq_and_group(*ids[:-1])
            return pt[sb, pt.shape[1] - 1 - (grp * pps + p)], 0, 0
        return pl.BlockSpec((1, width, page), index)

    const = lambda shape: pl.BlockSpec(shape, lambda *ids: (0, 0))
    args = [q.reshape(bs, n_q, width), pad(k_new), pad(v_new), bias_col, _tri2(page)] + [ck] * pps + [cv] * pps
    in_specs = ([per_seq(n_q), per_seq(SUBLANES), per_seq(SUBLANES), const((nrow, 1)), const((2 * page, page))]
                + [page_spec(p) for p in range(pps)] * 2)
    scratch = [pltpu.VMEM((nrow, width), BF16), pltpu.VMEM((nrow, width), F32),
               pltpu.VMEM((nrow, 1), F32), pltpu.VMEM((page, width), F32)]
    return args, in_specs, per_seq(n_q), jax.ShapeDtypeStruct((bs, n_q, width), F32), scratch


def _attn_sample(q, k_new, v_new, sb_bias, cache_k, cache_v, page_table, bs, n_q):
    n_heads = cache_k.shape[2]
    n_pages = page_table.shape[1]
    pps = _pick_tile(n_pages, SAMPLE_PAGE_GROUP, 1)
    args, in_specs, out_spec, out_shape, scratch = _sample_attn_operands(
        q, k_new, v_new, sb_bias, cache_k, cache_v, bs, n_q, pps, lambda b, s: (b, s))
    grid_spec = pltpu.PrefetchScalarGridSpec(
        num_scalar_prefetch=1, grid=(bs, n_pages // pps), in_specs=in_specs, out_specs=out_spec,
        scratch_shapes=scratch)
    out = pl.pallas_call(
        functools.partial(_attn_sample_kernel, n_heads=n_heads, n_q=n_q, pages_per_step=pps),
        grid_spec=grid_spec, out_shape=out_shape,
        compiler_params=_cparams(("arbitrary", "arbitrary")), name="attn_sample")(page_table, *args)
    return out.reshape(bs * n_q, -1)


def _attn_fused_kernel(pt_ref, bias_ref, q_ref, k_ref, v_ref, tri_ref, *rest, tq, tk, n_heads, n_q, pages_per_step,
                       steps_per_seq):
    del pt_ref
    sq_ref, kn_ref, vn_ref, sbias_ref, stri_ref = rest[:5]
    k_refs = rest[5:5 + pages_per_step]
    v_refs = rest[5 + pages_per_step:5 + 2 * pages_per_step]
    (o_ref, os_ref, kb_ref, vb_ref, acc_ref, ptail_ref,
     wq_ref, sacc_ref, tail_ref, pad_ref) = rest[5 + 2 * pages_per_step:]
    hp, qi, half = pl.program_id(1), pl.program_id(2), pl.program_id(3)
    lin = ((pl.program_id(0) * pl.num_programs(1) + hp) * pl.num_programs(2) + qi) * pl.num_programs(3) + half
    grp = lin % steps_per_seq
    _sample_attn_step(grp == 0, grp == steps_per_seq - 1, sq_ref, kn_ref, vn_ref, sbias_ref, stri_ref, k_refs, v_refs,
                      os_ref, wq_ref, sacc_ref, tail_ref, pad_ref, n_heads=n_heads, n_q=n_q)
    _prompt_attn_step(hp, qi, half, bias_ref, q_ref, k_ref, v_ref, tri_ref, o_ref, kb_ref, vb_ref, acc_ref, ptail_ref,
                      tq=tq, tk=tk)


def _attn_fused(proj, sb_bias, bsz, seq, width, q_s, k_new, v_new, cache_k, cache_v, page_table, bs, n_q):
    r = proj.shape[0]
    n_heads = cache_k.shape[2]
    n_pages = page_table.shape[1]
    tk = _pick_tile(seq, ATTN_K_TILE, LANES)
    tq = _pick_tile(seq, ATTN_Q_TILE, tk)
    tc = _pick_tile(tk, ATTN_CUMSUM_BLOCK, LANES)
    nq = seq // tq
    nhp = width // LANES
    halves = 2
    n_steps = bsz * nhp * nq * halves
    pps = (bs * n_pages) // n_steps
    if pps == 0 or pps * n_steps != bs * n_pages or n_pages % pps or pps > 2 * SAMPLE_PAGE_GROUP:
        return None
    sps = n_pages // pps

    def seq_and_group(b, hp, qi, half):
        lin = ((b * nhp + hp) * nq + qi) * halves + half
        return lin // sps, lin % sps

    s_args, s_specs, s_out_spec, s_out_shape, s_scratch = _sample_attn_operands(
        q_s, k_new, v_new, sb_bias, cache_k, cache_v, bs, n_q, pps, seq_and_group)
    whole_seq = lambda c0: pl.BlockSpec((seq, LANES), lambda b, hp, qi, half, pt: (b, c0 + hp),
                                        pipeline_mode=pl.Buffered(1))
    q_tile = pl.BlockSpec((tq, LANES), lambda b, hp, qi, half, pt: (b * nq + qi, hp))
    grid_spec = pltpu.PrefetchScalarGridSpec(
        num_scalar_prefetch=1, grid=(bsz, nhp, nq, halves),
        in_specs=[pl.BlockSpec(memory_space=pltpu.SMEM), q_tile, whole_seq(nhp), whole_seq(2 * nhp),
                  pl.BlockSpec((2 * tc, tc), lambda b, hp, qi, half, pt: (0, 0))] + s_specs,
        out_specs=[q_tile, s_out_spec],
        scratch_shapes=[pltpu.VMEM((seq, LANES), BF16), pltpu.VMEM((seq, LANES), BF16),
                        pltpu.VMEM((HEADS_PER_BLOCK, tq, LANES), F32),
                        pltpu.VMEM((HEADS_PER_BLOCK, tq, 1), F32)] + s_scratch)
    o_p, o_s = pl.pallas_call(
        functools.partial(_attn_fused_kernel, tq=tq, tk=tk, n_heads=n_heads, n_q=n_q, pages_per_step=pps,
                          steps_per_seq=sps),
        grid_spec=grid_spec, out_shape=[jax.ShapeDtypeStruct((r, width), F32), s_out_shape],
        compiler_params=_cparams(("arbitrary",) * 4), name="attn_fused")(
            page_table, sb_bias, proj, proj, proj, _tri2(tc), *s_args)
    return o_p, o_s.reshape(bs * n_q, -1)


def _ssm_kernel(*refs, chunks_per_seq, per_row_h0):
    if per_row_h0:
        u_ref, bw_ref, cw_ref, k_ref, d_ref, h0_ref, gg_ref, hs_ref, bu_ref = refs
    else:
        u_ref, bw_ref, cw_ref, k_ref, d_ref, gg_ref, hs_ref, bu_ref, carry_ref = refs
    i, j = pl.program_id(0), pl.program_id(1)
    rows = u_ref.shape[0]
    half = bu_ref.shape[1] // 2
    u = u_ref[...]
    bu_ref[...] = _dot(u.astype(BF16), bw_ref[0])

    if not per_row_h0:
        @pl.when(i % chunks_per_seq == 0)
        def _():
            carry_ref[j] = jnp.zeros((1, 2 * half), F32)

    def tile(tix, carry):
        r0 = pl.multiple_of(tix * SUBLANES, SUBLANES)
        br = bu_ref[pl.ds(r0, SUBLANES), 0:half]
        bi = bu_ref[pl.ds(r0, SUBLANES), half:2 * half]
        for n, d in enumerate((1, 2, 4)):
            ar, ai = k_ref[0, 2 * n], k_ref[0, 2 * n + 1]
            sr, si = pltpu.roll(br, d, 0), pltpu.roll(bi, d, 0)
            br, bi = br + ar * sr - ai * si, bi + ar * si + ai * sr
        pr, pi = k_ref[0, 6], k_ref[0, 7]
        if per_row_h0:
            cr = h0_ref[pl.ds(r0, SUBLANES), 0:half]
            ci = h0_ref[pl.ds(r0, SUBLANES), half:2 * half]
        else:
            cr, ci = carry
        hr = br + pr * cr - pi * ci
        hi = bi + pr * ci + pi * cr
        bu_ref[pl.ds(r0, SUBLANES), 0:half] = hr
        bu_ref[pl.ds(r0, SUBLANES), half:2 * half] = hi
        return (hr[SUBLANES - 1:SUBLANES], hi[SUBLANES - 1:SUBLANES])

    if per_row_h0:
        init = (jnp.zeros((1, half), F32),) * 2
    else:
        init = (carry_ref[j, :, 0:half], carry_ref[j, :, half:2 * half])
    last = lax.fori_loop(0, rows // SUBLANES, tile, init)
    if per_row_h0:
        hs_ref[...] = bu_ref[...]
    else:
        carry_ref[j, :, 0:half] = last[0]
        carry_ref[j, :, half:2 * half] = last[1]
        hs_ref[0, 0] = bu_ref[rows - SUBLANES:rows, :]
    y = _dot(bu_ref[...].astype(BF16), cw_ref[0]) + d_ref[...] * u
    gg_ref[...] = _gelu(y)


def _ssm(proj, u_col, bblk, cblk, consts, d_skip, seq, tc, h0_rows=None):
    r = proj.shape[0]
    width = d_skip.shape[0]
    nb, cb, sw = bblk.shape
    per_row = h0_rows is not None
    cps = max(seq // tc, 1)
    u_blk = u_col // cb
    assert u_blk * cb == u_col
    in_specs = [pl.BlockSpec((tc, cb), lambda i, j: (i, u_blk + j)),
                pl.BlockSpec((1, cb, sw), lambda i, j: (j, 0, 0)),
                pl.BlockSpec((1, sw, cb), lambda i, j: (j, 0, 0)),
                pl.BlockSpec((1, 8, SUBLANES, sw // 2), lambda i, j: (j, 0, 0, 0)),
                pl.BlockSpec((1, cb), lambda i, j: (0, j))]
    args = [proj, bblk, cblk, consts, d_skip.reshape(1, width)]
    scratch = [pltpu.VMEM((tc, sw), F32)]
    if per_row:
        assert r == tc
        in_specs.append(pl.BlockSpec((tc, sw), lambda i, j: (0, j)))
        args.append(h0_rows)
        hs_shape = jax.ShapeDtypeStruct((r, nb * sw), F32)
        hs_spec = pl.BlockSpec((tc, sw), lambda i, j: (0, j))
    else:
        hs_shape = jax.ShapeDtypeStruct((r // tc, nb, SUBLANES, sw), F32)
        hs_spec = pl.BlockSpec((1, 1, SUBLANES, sw), lambda i, j: (i, j, 0, 0))
        scratch.append(pltpu.VMEM((nb, 1, sw), F32))
    return pl.pallas_call(
        functools.partial(_ssm_kernel, chunks_per_seq=cps, per_row_h0=per_row),
        grid=(r // tc, nb), in_specs=in_specs,
        out_specs=[pl.BlockSpec((tc, cb), lambda i, j: (i, j)), hs_spec],
        out_shape=[jax.ShapeDtypeStruct((r, width), F32), hs_shape],
        scratch_shapes=scratch,
        compiler_params=_cparams(("arbitrary", "arbitrary")), name="ssm")(*args)


def _mix_kernel(oa_ref, gg_ref, ga_ref, gs_ref, x_ref, wglu_ref, bglu_ref, wba_ref, wbs_ref, wout_ref, g_ref, x1_ref):
    gg = gg_ref[...]
    o_ssm = gg * _sigmoid(_dot(gg.astype(BF16), wglu_ref[...]) + bglu_ref[...])
    merged = (_sigmoid(ga_ref[...]) * _dot(oa_ref[...].astype(BF16), wba_ref[...])
              + _sigmoid(gs_ref[...]) * _dot(o_ssm.astype(BF16), wbs_ref[...]))
    mo = _dot(merged.astype(BF16), wout_ref[...])
    x1_ref[...] = x_ref[...] + _rms(mo) * g_ref[...]


def _mix(o_attn, gg, proj, gate_col, x, wglu, bglu, wba, wbs, wout, g, tm):
    r, d = x.shape
    w = o_attn.shape[1]
    gblk = gate_col // d
    assert gblk * d == gate_col
    row = lambda n, c=0: pl.BlockSpec((tm, n), lambda i: (i, c))
    const = lambda a, b: pl.BlockSpec((a, b), lambda i: (0, 0), pipeline_mode=pl.Buffered(1))
    return pl.pallas_call(
        _mix_kernel, grid=(r // tm,),
        in_specs=[row(w), row(w), row(d, gblk), row(d, gblk + 1), row(d), const(w, w), const(1, w), const(w, d),
                  const(w, d), const(d, d), const(1, d)],
        out_specs=row(d), out_shape=jax.ShapeDtypeStruct((r, d), F32),
        compiler_params=_cparams(("arbitrary",)), name="mix")(
            o_attn, gg, proj, proj, x, wglu, bglu.reshape(1, w), wba, wbs, wout, g.reshape(1, d))


def _ffn_kernel(*refs, seq, has_prev, conv_w):
    if has_prev:
        (x_ref, g3_ref, wg_ref, wv_ref, cw_ref, cb_ref, wd_ref, g4_ref, p1_ref, p2_ref,
         y_ref, cs_ref, xn_ref, acc_ref, gbuf_ref, carry_ref) = refs
    else:
        (x_ref, g3_ref, wg_ref, wv_ref, cw_ref, cb_ref, wd_ref, g4_ref,
         y_ref, cs_ref, xn_ref, acc_ref, gbuf_ref, carry_ref) = refs
    i, j = pl.program_id(0), pl.program_id(1)
    tm = x_ref.shape[0]
    assert conv_w == 3

    @pl.when(j == 0)
    def _():
        xn_ref[...] = (_rms(x_ref[...]) * g3_ref[...]).astype(BF16)
        acc_ref[...] = jnp.zeros_like(acc_ref)

    @pl.when(i == 0)
    def _():
        carry_ref[j] = jnp.zeros(carry_ref.shape[1:], F32)

    xn = xn_ref[...]
    t = (i * tm + lax.broadcasted_iota(jnp.int32, (tm, 1), 0)) % seq
    keep = cs_ref.shape[1]
    tn = wg_ref.shape[1]
    part = min(tn, FFN_COL_PART)
    for c0 in range(0, tn, part):
        cols = slice(c0, c0 + part)
        gate = _dot(xn, wg_ref[:, cols])
        val = _dot(xn, wv_ref[:, cols])
        gbuf_ref[0:SUBLANES, cols] = carry_ref[j, :, cols]
        gbuf_ref[SUBLANES:SUBLANES + tm, cols] = gate
        g1 = gbuf_ref[pl.ds(SUBLANES - 1, tm), cols]
        g2 = gbuf_ref[pl.ds(SUBLANES - 2, tm), cols]
        g1 = jnp.where(t >= 1, g1, p1_ref[:, cols] if has_prev else 0.0)
        g2 = jnp.where(t >= 2, g2, p2_ref[:, cols] if has_prev else 0.0)
        cw = cw_ref[:, cols]
        conv = cb_ref[:, cols] + cw[0:1] * g2 + cw[1:2] * g1 + cw[2:3] * gate
        h = _gelu(conv) * val
        acc_ref[...] += _dot(h.astype(BF16), wd_ref[cols, :])
        carry_ref[j, :, cols] = gate[tm - SUBLANES:tm]
        cs_ref[0, :, cols] = gate[tm - keep:tm]

    @pl.when(j == pl.num_programs(1) - 1)
    def _():
        y_ref[...] = x_ref[...] + _rms(acc_ref[...]) * g4_ref[...]


def _ffn(x1, g3, wg, wv, cw, cb, wd, g4, seq, tm, prev=None):
    r, d = x1.shape
    dffp = wg.shape[1]
    tn = FFN_COL_TILE
    nj = dffp // tn
    has_prev = prev is not None
    keep = tm if has_prev else SUBLANES
    in_specs = [pl.BlockSpec((tm, d), lambda i, j: (i, 0)),
                pl.BlockSpec((1, d), lambda i, j: (0, 0)),
                pl.BlockSpec((d, tn), lambda i, j: (0, j)),
                pl.BlockSpec((d, tn), lambda i, j: (0, j)),
                pl.BlockSpec((3, tn), lambda i, j: (0, j)),
                pl.BlockSpec((1, tn), lambda i, j: (0, j)),
                pl.BlockSpec((tn, d), lambda i, j: (j, 0)),
                pl.BlockSpec((1, d), lambda i, j: (0, 0))]
    args = [x1, g3.reshape(1, d), wg, wv, cw, cb.reshape(1, dffp), wd, g4.reshape(1, d)]
    if has_prev:
        assert r == tm
        in_specs += [pl.BlockSpec((tm, tn), lambda i, j: (0, j))] * 2
        args += list(prev)
    n_cs = r // tm
    return pl.pallas_call(
        functools.partial(_ffn_kernel, seq=seq, has_prev=has_prev, conv_w=cw.shape[0]),
        grid=(r // tm, nj), in_specs=in_specs,
        out_specs=[pl.BlockSpec((tm, d), lambda i, j: (i, 0)),
                   pl.BlockSpec((1, keep, tn), lambda i, j: (i, 0, j))],
        out_shape=[jax.ShapeDtypeStruct((r, d), F32), jax.ShapeDtypeStruct((n_cs, keep, dffp), F32)],
        scratch_shapes=[pltpu.VMEM((tm, d), BF16), pltpu.VMEM((tm, d), F32),
                        pltpu.VMEM((tm + SUBLANES, tn), F32), pltpu.VMEM((nj, SUBLANES, tn), F32)],
        compiler_params=_cparams(("arbitrary", "arbitrary")), name="ffn")(*args)


def _layer(xp, xs, cache_k, cache_v, page_table, h0_re, h0_im, conv_prev, wts):
    (norm_mix_pre, norm_mix_post, norm_ffn_pre, norm_ffn_post, w_in, sb_bias, a_re, a_im, log_dt, b_re, b_im,
     c_re, c_im, d_skip, w_glu, b_glu, w_ba, w_bs, w_out, w_up, conv_w, conv_b, w_down) = wts
    bp, seq, d = xp.shape
    bs, nq, _ = xs.shape
    n_heads = sb_bias.shape[0]
    aw = n_heads * HEAD_DIM
    g, p = a_re.shape
    c = b_re.shape[-1]
    sw_total = g * c
    gb = SSM_BLOCK_CH // c
    nb = g // gb
    dff = conv_w.shape[1]
    dffp = -(-dff // FFN_COL_TILE) * FFN_COL_TILE
    u_col, gate_col = 3 * aw, 3 * aw + sw_total
    assert w_in.shape[1] == gate_col + 2 * d

    w_in_b, w_glu_b, w_ba_b, w_bs_b, w_out_b = (a.astype(BF16) for a in (w_in, w_glu, w_ba, w_bs, w_out))
    padc = lambda a: jnp.pad(a, ((0, 0), (0, dffp - dff)))
    wg_b = padc(w_up[:, :dff]).astype(BF16)
    wv_b = padc(w_up[:, dff:]).astype(BF16)
    wd_b = jnp.pad(w_down, ((0, dffp - dff), (0, 0))).astype(BF16)
    cw_p, cb_p = padc(conv_w), jnp.pad(conv_b, (0, dffp - dff))

    powr, powi, bbr, bbi = _ssm_prep(a_re, a_im, log_dt, b_re, b_im)
    bblk, cblk = _ssm_block_weights(bbr, bbi, c_re, c_im)

    def states(h):
        h = h.reshape(-1, nb, 2, gb, p)
        return h[:, :, 0].reshape(-1, g, p), h[:, :, 1].reshape(-1, g, p)

    rp = bp * seq
    x2 = xp.reshape(rp, d)
    rs = bs * nq
    assert SUBLANES % nq == 0
    xs2 = xs.reshape(rs, d)
    proj = _in_proj(x2, norm_mix_pre, w_in_b, _pick_tile(rp, IN_PROJ_ROW_TILE))
    proj_s = _in_proj(xs2, norm_mix_pre, w_in_b, rs)
    k, v = proj[:, aw:2 * aw], proj[:, 2 * aw:3 * aw]
    qs, ks, vs = proj_s[:, :aw], proj_s[:, aw:2 * aw], proj_s[:, 2 * aw:3 * aw]
    fused = _attn_fused(proj, sb_bias, bp, seq, aw, qs, ks, vs, cache_k, cache_v, page_table, bs, nq)
    if fused is None:
        fused = (_attn_prompt(proj, sb_bias, bp, seq, aw),
                 _attn_sample(qs, ks, vs, sb_bias, cache_k, cache_v, page_table, bs, nq))
    o_attn, o_attn_s = fused
    tc = _pick_tile(seq, 512)
    gg, hs = _ssm(proj, u_col, bblk, cblk, _scan_consts(powr, powi, SUBLANES, gb), d_skip, seq, tc)
    hrp, hip = states(hs.reshape(bp, seq // tc, nb, SUBLANES, -1)[:, -1, :, SUBLANES - 1, :].reshape(bp, -1))
    x1 = _mix(o_attn, gg, proj, gate_col, x2, w_glu_b, b_glu, w_ba_b, w_bs_b, w_out_b, norm_mix_post,
              _pick_tile(rp, 256))
    tmf = _pick_tile(seq, 512)
    yp, csp = _ffn(x1, norm_ffn_pre, wg_b, wv_b, cw_p, cb_p, wd_b, norm_ffn_post, seq, tmf)
    out_p = (yp.reshape(bp, seq, d), k.reshape(bp, seq, n_heads, HEAD_DIM), v.reshape(bp, seq, n_heads, HEAD_DIM),
             hrp, hip, csp.reshape(bp, seq // tmf, SUBLANES, dffp)[:, -1, SUBLANES - (conv_w.shape[0] - 1):, :dff])

    proj, o_attn, k, v = proj_s, o_attn_s, ks, vs
    h0 = jnp.stack([h0_re.reshape(bs, nb, gb * p), h0_im.reshape(bs, nb, gb * p)], axis=2).reshape(bs, nb * 2 * gb * p)
    gg, hs = _ssm(proj, u_col, bblk, cblk, _scan_consts(powr, powi, nq, gb), d_skip, nq, rs,
                  h0_rows=jnp.repeat(h0, nq, axis=0))
    hrs, his = states(hs.reshape(bs, nq, -1)[:, nq - 1])
    x1 = _mix(o_attn, gg, proj, gate_col, xs2, w_glu_b, b_glu, w_ba_b, w_bs_b, w_out_b, norm_mix_post, rs)
    t_row = jnp.arange(rs) % nq
    prev = padc(conv_prev.reshape(bs * (conv_w.shape[0] - 1), dff)).reshape(bs, conv_w.shape[0] - 1, dffp)
    prev_rows = jnp.repeat(prev, nq, axis=0)
    p1 = jnp.where((t_row == 0)[:, None], prev_rows[:, 1], 0.0)
    p2 = jnp.where((t_row == 0)[:, None], prev_rows[:, 0], jnp.where((t_row == 1)[:, None], prev_rows[:, 1], 0.0))
    ys, css = _ffn(x1, norm_ffn_pre, wg_b, wv_b, cw_p, cb_p, wd_b, norm_ffn_post, nq, rs, prev=(p1, p2))
    css = css.reshape(bs, nq, dffp)[:, nq - (conv_w.shape[0] - 1):, :dff]
    out_s = (ys.reshape(bs, nq, d), k.reshape(bs, nq, n_heads, HEAD_DIM), v.reshape(bs, nq, n_heads, HEAD_DIM),
             hrs, his, css)
    return out_p, out_s


def kernel(x_prompt, x_sample, cache_k, cache_v, page_table, state_ssm_re, state_ssm_im, state_ffn_conv, norm_mix_pre, norm_mix_post, norm_ffn_pre, norm_ffn_post, w_in, sb_bias, ssm_a_re, ssm_a_im, ssm_log_dt, ssm_b_re, ssm_b_im, ssm_c_re, ssm_c_im, ssm_d, ssm_w_glu, ssm_b_glu, w_branch_attn, w_branch_ssm, w_out, w_up, conv_w, conv_b, w_down):
    depth = w_in.shape[0]
    xp, xs = x_prompt, x_sample
    acc_p, acc_s = [], []
    for layer in range(depth):
        wts = tuple(a[layer] for a in (
            norm_mix_pre, norm_mix_post, norm_ffn_pre, norm_ffn_post, w_in, sb_bias, ssm_a_re, ssm_a_im, ssm_log_dt,
            ssm_b_re, ssm_b_im, ssm_c_re, ssm_c_im, ssm_d, ssm_w_glu, ssm_b_glu, w_branch_attn, w_branch_ssm, w_out,
            w_up, conv_w, conv_b, w_down))
        out_p, out_s = _layer(xp, xs, cache_k[layer], cache_v[layer], page_table, state_ssm_re[layer],
                              state_ssm_im[layer], state_ffn_conv[layer], wts)
        xp, xs = out_p[0], out_s[0]
        acc_p.append(out_p[1:])
        acc_s.append(out_s[1:])
    stack = lambda acc, n: jnp.stack([a[n] for a in acc])
    kp, vp, hrp, hip, cp = (stack(acc_p, n) for n in range(5))
    ks, vs, hrs, his, cs = (stack(acc_s, n) for n in range(5))
    return (xp, xs, kp, vp, ks, vs, hrp, hip, hrs, his, cp, cs)
```

```python
import functools
import math

import jax
import jax.numpy as jnp
from jax import lax
from jax.experimental import pallas as pl
from jax.experimental.pallas import tpu as pltpu

RMS_EPS = 1e-6
LOG2E = math.log2(math.e)
HEAD_DIM = 64
HEADS_PER_BLOCK = 2
LANES = 128
SUBLANES = 8
SSM_BLOCK_CH = 128
VMEM_LIMIT = 56 * 1024 * 1024
ATTN_Q_TILE = 512
ATTN_K_TILE = 512
ATTN_CUMSUM_BLOCK = 256
SAMPLE_PAGE_GROUP = 16
FFN_COL_TILE = 512
FFN_COL_PART = 512
IN_PROJ_COL_TILE = 1024
IN_PROJ_ROW_TILE = 1024
BF16 = jnp.bfloat16
F32 = jnp.float32


def _cparams(sem):
    return pltpu.CompilerParams(dimension_semantics=sem, vmem_limit_bytes=VMEM_LIMIT)


def _pick_tile(n, target, quantum=SUBLANES):
    best = None
    for t in range(quantum, min(n, target) + 1, quantum):
        if n % t == 0:
            best = t
    assert best is not None, (n, target, quantum)
    return best


def _dot(a, b):
    return jnp.dot(a, b, preferred_element_type=F32)


def _dot_nt(a, b):
    return lax.dot_general(a, b, (((1,), (1,)), ((), ())), preferred_element_type=F32)


def _sigmoid(x):
    return 1.0 / (1.0 + jnp.exp(-x))


def _gelu(x):
    c = math.sqrt(2.0 / math.pi)
    return 0.5 * x * (1.0 + jnp.tanh(x * (c + (c * 0.044715) * (x * x))))


def _rms(x):
    return x * lax.rsqrt(jnp.mean(x * x, axis=-1, keepdims=True) + RMS_EPS)


def _softplus(z):
    return jnp.maximum(z, 0.0) + jnp.log(1.0 + jnp.exp2(jnp.abs(z) * (-LOG2E)))


def _ssm_discretise(ar, ai, ldt):
    dt = jnp.exp(ldt)
    mag = jnp.exp(dt * ar)
    return mag * jnp.cos(dt * ai), mag * jnp.sin(dt * ai)


def _ssm_prep_kernel(ar_ref, ai_ref, ldt_ref, ar3_ref, ai3_ref, ldt3_ref, br_ref, bi_ref,
                     powr_ref, powi_ref, bbr_ref, bbi_ref):
    ar, ai = ar3_ref[...], ai3_ref[...]
    abr, abi = _ssm_discretise(ar, ai, ldt3_ref[...])
    nr, ni = abr - 1.0, abi
    den = ar * ar + ai * ai
    fr, fi = (nr * ar + ni * ai) / den, (ni * ar - nr * ai) / den
    br, bi = br_ref[...], bi_ref[...]
    bbr_ref[...] = fr * br - fi * bi
    bbi_ref[...] = fr * bi + fi * br
    abr, abi = _ssm_discretise(ar_ref[...], ai_ref[...], ldt_ref[...])
    pr, pi = abr, abi
    powr_ref[0], powi_ref[0] = pr, pi
    for k in range(1, SUBLANES):
        pr, pi = pr * abr - pi * abi, pr * abi + pi * abr
        powr_ref[k], powi_ref[k] = pr, pi


def _ssm_prep(a_re, a_im, log_dt, b_re, b_im):
    g, p = a_re.shape
    c = b_re.shape[-1]
    out = (jax.ShapeDtypeStruct((SUBLANES, g, p), F32),) * 2 + (jax.ShapeDtypeStruct((g, c, p), F32),) * 2
    return pl.pallas_call(_ssm_prep_kernel, out_shape=out, name="ssm_prep")(
        a_re, a_im, log_dt.reshape(g, 1), a_re.reshape(g, 1, p), a_im.reshape(g, 1, p), log_dt.reshape(g, 1, 1),
        b_re.transpose(0, 2, 1), b_im.transpose(0, 2, 1))


def _ssm_block_weights(bbr, bbi, c_re, c_im):
    g, c, p = bbr.shape
    gb = SSM_BLOCK_CH // c
    nb = g // gb
    eye = jnp.eye(gb, dtype=F32)
    bb = jnp.stack([bbr, bbi], axis=0).reshape(2, nb, gb, c, p)
    bblk = jnp.einsum("rjgcp,gh->jgcrhp", bb, eye).reshape(nb, gb * c, 2 * gb * p)
    cc = jnp.stack([c_re, -c_im], axis=0).reshape(2, nb, gb, c, p)
    cblk = jnp.einsum("rjgcp,gh->jrgphc", cc, eye).reshape(nb, 2 * gb * p, gb * c)
    return bblk.astype(BF16), cblk.astype(BF16)


def _scan_consts(powr, powi, seg, gb):
    _, g, p = powr.shape
    nb = g // gb
    rr = jnp.arange(SUBLANES) % seg
    kinds = []
    for d in (1, 2, 4):
        m = (rr >= d).astype(F32)[:, None, None]
        kinds += [m * powr[d - 1][None], m * powi[d - 1][None]]
    kinds += [powr[rr], powi[rr]]
    k = jnp.stack(kinds, axis=0)
    return k.reshape(8, SUBLANES, nb, gb * p).transpose(2, 0, 1, 3)


def _inproj_kernel(x_ref, g_ref, w_ref, o_ref, xn_ref):
    @pl.when(pl.program_id(1) == 0)
    def _():
        xn_ref[...] = (_rms(x_ref[...]) * g_ref[...]).astype(BF16)

    o_ref[...] = _dot(xn_ref[...], w_ref[...])


def _in_proj(x, g, w_bf16, tm):
    r, d = x.shape
    n = w_bf16.shape[1]
    tn = _pick_tile(n, IN_PROJ_COL_TILE, LANES)
    return pl.pallas_call(
        _inproj_kernel, grid=(r // tm, n // tn),
        in_specs=[pl.BlockSpec((tm, d), lambda i, j: (i, 0)),
                  pl.BlockSpec((1, d), lambda i, j: (0, 0)),
                  pl.BlockSpec((d, tn), lambda i, j: (0, j))],
        out_specs=pl.BlockSpec((tm, tn), lambda i, j: (i, j)),
        out_shape=jax.ShapeDtypeStruct((r, n), F32),
        scratch_shapes=[pltpu.VMEM((tm, d), BF16)],
        compiler_params=_cparams(("arbitrary", "arbitrary")), name="in_proj")(x, g.reshape(1, d), w_bf16)


def _sb_weights(z, mask, tail, tri2, latest_first):
    nk = tri2.shape[1]
    n = z.shape[1] // nk
    sp = _softplus(z)
    if mask is not None:
        sp = jnp.where(mask, sp, 0.0)
    hi = sp.astype(BF16)
    lo = (sp - hi.astype(F32)).astype(BF16)
    parts = [None] * n
    for c in (range(n) if latest_first else reversed(range(n))):
        cols = slice(c * nk, (c + 1) * nk)
        local = _dot(jnp.concatenate([hi[:, cols], lo[:, cols]], axis=1), tri2)
        parts[c] = local + tail
        tail = tail + local[:, 0:1]
    rsum = parts[0] if n == 1 else jnp.concatenate(parts, axis=1)
    w = jnp.exp(z - rsum)
    if mask is not None:
        w = jnp.where(mask, w, 0.0)
    return w.astype(BF16), tail


def _tri2(n):
    tri = (jnp.arange(n)[:, None] >= jnp.arange(n)[None, :]).astype(BF16)
    return jnp.concatenate([tri, tri], axis=0)


def _prompt_attn_step(hp, qi, half, bias_ref, q_ref, k_ref, v_ref, tri_ref, o_ref, kb_ref, vb_ref, acc_ref, znext_ref,
                      tail_ref, *, tq, tk):
    assert tq % tk == 0 or tk % tq == 0
    nd = max(tq // tk, 1)
    first_diag = (qi * tq) // tk

    lane = lax.broadcasted_iota(jnp.int32, (1, LANES), 1)
    first = lane < HEAD_DIM
    q = q_ref[...] * (HEAD_DIM ** -0.5)
    qh = (jnp.where(first, q, 0.0).astype(BF16), jnp.where(first, 0.0, q).astype(BF16))
    bias = (bias_ref[HEADS_PER_BLOCK * hp], bias_ref[HEADS_PER_BLOCK * hp + 1])
    tri2 = tri_ref[...]
    def logits(j, h):
        start = pl.multiple_of(j * tk, tk)
        return _dot_nt(qh[h], kb_ref[pl.ds(start, tk), :]) + bias[h]

    def weights(z, j, tail, masked):
        mask = None
        if masked:
            qpos = qi * tq + lax.broadcasted_iota(jnp.int32, (tq, tk), 0)
            mask = j * tk + lax.broadcasted_iota(jnp.int32, (tq, tk), 1) < qpos
        return _sb_weights(z, mask, tail, tri2, latest_first=False)

    def times_v(w, j):
        return _dot(w, vb_ref[pl.ds(pl.multiple_of(j * tk, tk), tk), :])

    assert HEADS_PER_BLOCK == 2

    def body(it, tails):
        j = first_diag - 1 - it
        z1 = znext_ref[...]
        z0 = logits(j, 0)
        w1, t1 = weights(z1, j, tails[1], False)
        acc_ref[1] += times_v(w1, j)
        w0, t0 = weights(z0, j, tails[0], False)
        acc_ref[0] += times_v(w0, j)
        znext_ref[...] = logits(jnp.maximum(j - 1, 0), 1)
        return (t0, t1)

    def below_diagonal(lo, hi, tails):
        return lax.fori_loop(lo, hi, body, tails)

    def start(n_off):
        @pl.when(qi == 0)
        def _():
            kb_ref[...] = k_ref[...].astype(BF16)
            vb_ref[...] = v_ref[...].astype(BF16)

        tails = [jnp.zeros((tq, 1), F32)] * HEADS_PER_BLOCK
        for dd in reversed(range(nd)):
            for h in range(HEADS_PER_BLOCK):
                j = first_diag + dd
                w, tails[h] = weights(logits(j, h), j, tails[h], True)
                if dd == nd - 1:
                    acc_ref[h] = times_v(w, j)
                else:
                    acc_ref[h] += times_v(w, j)
        znext_ref[...] = logits(jnp.maximum(first_diag - 1, 0), 1)
        return below_diagonal(0, n_off, tuple(tails))

    def finish(lo, tails):
        below_diagonal(lo, first_diag, tails)
        o_ref[...] = jnp.where(first, acc_ref[0], acc_ref[1])

    if half is None:
        finish(first_diag, start(first_diag))
    else:
        split = first_diag // 2

        @pl.when(half == 0)
        def _():
            tails = start(split)
            for h in range(HEADS_PER_BLOCK):
                tail_ref[h] = tails[h]

        @pl.when(half == 1)
        def _():
            finish(split, tuple(tail_ref[h] for h in range(HEADS_PER_BLOCK)))


def _attn_prompt_kernel(bias_ref, q_ref, k_ref, v_ref, tri_ref, o_ref, kb_ref, vb_ref, acc_ref, znext_ref, *, tq, tk):
    _prompt_attn_step(pl.program_id(1), pl.program_id(2), None, bias_ref, q_ref, k_ref, v_ref, tri_ref, o_ref,
                      kb_ref, vb_ref, acc_ref, znext_ref, None, tq=tq, tk=tk)


def _attn_prompt(proj, sb_bias, bsz, seq, width):
    r = proj.shape[0]
    tk = _pick_tile(seq, ATTN_K_TILE, LANES)
    tq = _pick_tile(seq, ATTN_Q_TILE, tk)
    tc = _pick_tile(tk, ATTN_CUMSUM_BLOCK, LANES)
    nq = seq // tq
    nhp = width // LANES
    return pl.pallas_call(
        functools.partial(_attn_prompt_kernel, tq=tq, tk=tk),
        grid=(bsz, nhp, nq),
        in_specs=[pl.BlockSpec(memory_space=pltpu.SMEM),
                  pl.BlockSpec((tq, LANES), lambda b, hp, qi: (b * nq + qi, hp)),
                  pl.BlockSpec((seq, LANES), lambda b, hp, qi: (b, nhp + hp)),
                  pl.BlockSpec((seq, LANES), lambda b, hp, qi: (b, 2 * nhp + hp)),
                  pl.BlockSpec((2 * tc, tc), lambda b, hp, qi: (0, 0))],
        out_specs=pl.BlockSpec((tq, LANES), lambda b, hp, qi: (b * nq + qi, hp)),
        out_shape=jax.ShapeDtypeStruct((r, width), F32),
        scratch_shapes=[pltpu.VMEM((seq, LANES), BF16), pltpu.VMEM((seq, LANES), BF16),
                        pltpu.VMEM((HEADS_PER_BLOCK, tq, LANES), F32), pltpu.VMEM((tq, tk), F32)],
        compiler_params=_cparams(("arbitrary", "arbitrary", "arbitrary")), name="attn_prompt")(
            sb_bias, proj, proj, proj, _tri2(tc))


def _sample_attn_step(is_first, is_last, q_ref, kn_ref, vn_ref, bias_ref, tri_ref, k_refs, v_refs,
                      o_ref, wq_ref, acc_ref, tail_ref, pad_ref, *, n_heads, n_q):
    nrow = n_q * n_heads
    width = n_heads * HEAD_DIM
    assert n_heads & (n_heads - 1) == 0 and HEAD_DIM & (HEAD_DIM - 1) == 0
    head_shift, dim_shift = n_heads.bit_length() - 1, HEAD_DIM.bit_length() - 1
    r_head = lax.broadcasted_iota(jnp.int32, (nrow, width), 0) & (n_heads - 1)
    c_head = lax.broadcasted_iota(jnp.int32, (nrow, width), 1) >> dim_shift
    own_head = r_head == c_head
    page = tri_ref.shape[1]
    tri2 = tri_ref[...]
    bias = bias_ref[...]

    @pl.when(is_first)
    def _():
        q = q_ref[0] * (HEAD_DIM ** -0.5)
        rows = jnp.concatenate([jnp.broadcast_to(q[i:i + 1], (n_heads, width)) for i in range(n_q)], axis=0)
        wq_ref[...] = jnp.where(own_head, rows, 0.0).astype(BF16)
        pad_ref[...] = jnp.zeros_like(pad_ref)
        pad_ref[0:SUBLANES, :] = kn_ref[0]
        z = _dot_nt(wq_ref[...], pad_ref[...].astype(BF16)) + bias
        pad_ref[0:SUBLANES, :] = vn_ref[0]
        kpos = lax.broadcasted_iota(jnp.int32, (nrow, page), 1)
        qpos = lax.broadcasted_iota(jnp.int32, (nrow, page), 0) >> head_shift
        w, tl = _sb_weights(z, kpos < qpos, jnp.zeros((nrow, 1), F32), tri2, latest_first=True)
        acc_ref[...] = _dot(w, pad_ref[...].astype(BF16))
        tail_ref[...] = tl

    tail = tail_ref[...]
    acc = acc_ref[...]
    for g0 in range(0, len(k_refs), SAMPLE_PAGE_GROUP):
        kcat = jnp.concatenate([r[0].astype(BF16) for r in k_refs[g0:g0 + SAMPLE_PAGE_GROUP]], axis=1)
        vcat = jnp.concatenate([r[0].astype(BF16) for r in v_refs[g0:g0 + SAMPLE_PAGE_GROUP]], axis=1)
        z = _dot(wq_ref[...], kcat) + bias
        w, tail = _sb_weights(z, None, tail, tri2, latest_first=True)
        acc = acc + _dot_nt(w, vcat)
    acc_ref[...] = acc
    tail_ref[...] = tail

    @pl.when(is_last)
    def _():
        own = jnp.where(own_head, acc, 0.0)
        o_ref[0] = jnp.concatenate(
            [jnp.sum(own[i * n_heads:(i + 1) * n_heads], axis=0, keepdims=True) for i in range(n_q)], axis=0)


def _attn_sample_kernel(pt_ref, q_ref, kn_ref, vn_ref, bias_ref, tri_ref, *rest, n_heads, n_q, pages_per_step):
    del pt_ref
    k_refs = rest[:pages_per_step]
    v_refs = rest[pages_per_step:2 * pages_per_step]
    s = pl.program_id(1)
    _sample_attn_step(s == 0, s == pl.num_programs(1) - 1, q_ref, kn_ref, vn_ref, bias_ref, tri_ref, k_refs, v_refs,
                      *rest[2 * pages_per_step:], n_heads=n_heads, n_q=n_q)


def _sample_attn_operands(q, k_new, v_new, sb_bias, cache_k, cache_v, bs, n_q, pps, seq_and_group):
    n_pool, page, n_heads, hd = cache_k.shape
    width = n_heads * hd
    nrow = n_q * n_heads
    ck = cache_k.transpose(0, 2, 3, 1).reshape(n_pool, width, page)
    cv = cache_v.transpose(0, 2, 3, 1).reshape(n_pool, width, page)
    pad = lambda a: jnp.pad(a.reshape(bs, n_q, width), ((0, 0), (0, SUBLANES - n_q), (0, 0)))
    bias_col = jnp.tile(sb_bias, n_q).reshape(nrow, 1)

    def per_seq(rows):
        return pl.BlockSpec((1, rows, width), lambda *ids: (seq_and_group(*ids[:-1])[0], 0, 0))

    def page_spec(p):
        def index(*ids):
            pt = ids[-1]
            sb, grp = seq_and_group(*ids[:-1])
            return pt[sb, pt.shape[1] - 1 - (grp * pps + p)], 0, 0
        return pl.BlockSpec((1, width, page), index)

    const = lambda shape: pl.BlockSpec(shape, lambda *ids: (0, 0))
    args = [q.reshape(bs, n_q, width), pad(k_new), pad(v_new), bias_col, _tri2(page)] + [ck] * pps + [cv] * pps
    in_specs = ([per_seq(n_q), per_seq(SUBLANES), per_seq(SUBLANES), const((nrow, 1)), const((2 * page, page))]
                + [page_spec(p) for p in range(pps)] * 2)
    scratch = [pltpu.VMEM((nrow, width), BF16), pltpu.VMEM((nrow, width), F32),
               pltpu.VMEM((nrow, 1), F32), pltpu.VMEM((page, width), F32)]
    return args, in_specs, per_seq(n_q), jax.ShapeDtypeStruct((bs, n_q, width), F32), scratch


def _attn_sample(q, k_new, v_new, sb_bias, cache_k, cache_v, page_table, bs, n_q):
    n_heads = cache_k.shape[2]
    n_pages = page_table.shape[1]
    pps = _pick_tile(n_pages, SAMPLE_PAGE_GROUP, 1)
    args, in_specs, out_spec, out_shape, scratch = _sample_attn_operands(
        q, k_new, v_new, sb_bias, cache_k, cache_v, bs, n_q, pps, lambda b, s: (b, s))
    grid_spec = pltpu.PrefetchScalarGridSpec(
        num_scalar_prefetch=1, grid=(bs, n_pages // pps), in_specs=in_specs, out_specs=out_spec,
        scratch_shapes=scratch)
    out = pl.pallas_call(
        functools.partial(_attn_sample_kernel, n_heads=n_heads, n_q=n_q, pages_per_step=pps),
        grid_spec=grid_spec, out_shape=out_shape,
        compiler_params=_cparams(("arbitrary", "arbitrary")), name="attn_sample")(page_table, *args)
    return out.reshape(bs * n_q, -1)


def _attn_fused_kernel(pt_ref, bias_ref, q_ref, k_ref, v_ref, tri_ref, *rest, tq, tk, n_heads, n_q, pages_per_step,
                       steps_per_seq):
    del pt_ref
    sq_ref, kn_ref, vn_ref, sbias_ref, stri_ref = rest[:5]
    k_refs = rest[5:5 + pages_per_step]
    v_refs = rest[5 + pages_per_step:5 + 2 * pages_per_step]
    (o_ref, os_ref, kb_ref, vb_ref, acc_ref, znext_ref, ptail_ref,
     wq_ref, sacc_ref, tail_ref, pad_ref) = rest[5 + 2 * pages_per_step:]
    hp, qi, half = pl.program_id(1), pl.program_id(2), pl.program_id(3)
    lin = ((pl.program_id(0) * pl.num_programs(1) + hp) * pl.num_programs(2) + qi) * pl.num_programs(3) + half
    grp = lin % steps_per_seq
    _sample_attn_step(grp == 0, grp == steps_per_seq - 1, sq_ref, kn_ref, vn_ref, sbias_ref, stri_ref, k_refs, v_refs,
                      os_ref, wq_ref, sacc_ref, tail_ref, pad_ref, n_heads=n_heads, n_q=n_q)
    _prompt_attn_step(hp, qi, half, bias_ref, q_ref, k_ref, v_ref, tri_ref, o_ref, kb_ref, vb_ref, acc_ref, znext_ref,
                      ptail_ref, tq=tq, tk=tk)


def _attn_fused(proj, sb_bias, bsz, seq, width, q_s, k_new, v_new, cache_k, cache_v, page_table, bs, n_q):
    r = proj.shape[0]
    n_heads = cache_k.shape[2]
    n_pages = page_table.shape[1]
    tk = _pick_tile(seq, ATTN_K_TILE, LANES)
    tq = _pick_tile(seq, ATTN_Q_TILE, tk)
    tc = _pick_tile(tk, ATTN_CUMSUM_BLOCK, LANES)
    nq = seq // tq
    nhp = width // LANES
    halves = 2
    n_steps = bsz * nhp * nq * halves
    pps = (bs * n_pages) // n_steps
    if pps == 0 or pps * n_steps != bs * n_pages or n_pages % pps or pps > 2 * SAMPLE_PAGE_GROUP:
        return None
    sps = n_pages // pps

    def seq_and_group(b, hp, qi, half):
        lin = ((b * nhp + hp) * nq + qi) * halves + half
        return lin // sps, lin % sps

    s_args, s_specs, s_out_spec, s_out_shape, s_scratch = _sample_attn_operands(
        q_s, k_new, v_new, sb_bias, cache_k, cache_v, bs, n_q, pps, seq_and_group)
    whole_seq = lambda c0: pl.BlockSpec((seq, LANES), lambda b, hp, qi, half, pt: (b, c0 + hp),
                                        pipeline_mode=pl.Buffered(1))
    q_tile = pl.BlockSpec((tq, LANES), lambda b, hp, qi, half, pt: (b * nq + qi, hp))
    grid_spec = pltpu.PrefetchScalarGridSpec(
        num_scalar_prefetch=1, grid=(bsz, nhp, nq, halves),
        in_specs=[pl.BlockSpec(memory_space=pltpu.SMEM), q_tile, whole_seq(nhp), whole_seq(2 * nhp),
                  pl.BlockSpec((2 * tc, tc), lambda b, hp, qi, half, pt: (0, 0))] + s_specs,
        out_specs=[q_tile, s_out_spec],
        scratch_shapes=[pltpu.VMEM((seq, LANES), BF16), pltpu.VMEM((seq, LANES), BF16),
                        pltpu.VMEM((HEADS_PER_BLOCK, tq, LANES), F32), pltpu.VMEM((tq, tk), F32),
                        pltpu.VMEM((HEADS_PER_BLOCK, tq, 1), F32)] + s_scratch)
    o_p, o_s = pl.pallas_call(
        functools.partial(_attn_fused_kernel, tq=tq, tk=tk, n_heads=n_heads, n_q=n_q, pages_per_step=pps,
                          steps_per_seq=sps),
        grid_spec=grid_spec, out_shape=[jax.ShapeDtypeStruct((r, width), F32), s_out_shape],
        compiler_params=_cparams(("arbitrary",) * 4), name="attn_fused")(
            page_table, sb_bias, proj, proj, proj, _tri2(tc), *s_args)
    return o_p, o_s.reshape(bs * n_q, -1)


def _ssm_kernel(*refs, chunks_per_seq, per_row_h0):
    if per_row_h0:
        u_ref, bw_ref, cw_ref, k_ref, d_ref, h0_ref, gg_ref, hs_ref, bu_ref = refs
    else:
        u_ref, bw_ref, cw_ref, k_ref, d_ref, gg_ref, hs_ref, bu_ref, carry_ref = refs
    i, j = pl.program_id(0), pl.program_id(1)
    rows = u_ref.shape[0]
    half = bu_ref.shape[1] // 2
    u = u_ref[...]
    bu_ref[...] = _dot(u.astype(BF16), bw_ref[0])

    if not per_row_h0:
        @pl.when(i % chunks_per_seq == 0)
        def _():
            carry_ref[j] = jnp.zeros((1, 2 * half), F32)

    def tile(tix, carry):
        r0 = pl.multiple_of(tix * SUBLANES, SUBLANES)
        br = bu_ref[pl.ds(r0, SUBLANES), 0:half]
        bi = bu_ref[pl.ds(r0, SUBLANES), half:2 * half]
        for n, d in enumerate((1, 2, 4)):
            ar, ai = k_ref[0, 2 * n], k_ref[0, 2 * n + 1]
            sr, si = pltpu.roll(br, d, 0), pltpu.roll(bi, d, 0)
            br, bi = br + ar * sr - ai * si, bi + ar * si + ai * sr
        pr, pi = k_ref[0, 6], k_ref[0, 7]
        if per_row_h0:
            cr = h0_ref[pl.ds(r0, SUBLANES), 0:half]
            ci = h0_ref[pl.ds(r0, SUBLANES), half:2 * half]
        else:
            cr, ci = carry
        hr = br + pr * cr - pi * ci
        hi = bi + pr * ci + pi * cr
        bu_ref[pl.ds(r0, SUBLANES), 0:half] = hr
        bu_ref[pl.ds(r0, SUBLANES), half:2 * half] = hi
        return (hr[SUBLANES - 1:SUBLANES], hi[SUBLANES - 1:SUBLANES])

    if per_row_h0:
        init = (jnp.zeros((1, half), F32),) * 2
    else:
        init = (carry_ref[j, :, 0:half], carry_ref[j, :, half:2 * half])
    last = lax.fori_loop(0, rows // SUBLANES, tile, init)
    if per_row_h0:
        hs_ref[...] = bu_ref[...]
    else:
        carry_ref[j, :, 0:half] = last[0]
        carry_ref[j, :, half:2 * half] = last[1]
        hs_ref[0, 0] = bu_ref[rows - SUBLANES:rows, :]
    y = _dot(bu_ref[...].astype(BF16), cw_ref[0]) + d_ref[...] * u
    gg_ref[...] = _gelu(y)


def _ssm(proj, u_col, bblk, cblk, consts, d_skip, seq, tc, h0_rows=None):
    r = proj.shape[0]
    width = d_skip.shape[0]
    nb, cb, sw = bblk.shape
    per_row = h0_rows is not None
    cps = max(seq // tc, 1)
    u_blk = u_col // cb
    assert u_blk * cb == u_col
    in_specs = [pl.BlockSpec((tc, cb), lambda i, j: (i, u_blk + j)),
                pl.BlockSpec((1, cb, sw), lambda i, j: (j, 0, 0)),
                pl.BlockSpec((1, sw, cb), lambda i, j: (j, 0, 0)),
                pl.BlockSpec((1, 8, SUBLANES, sw // 2), lambda i, j: (j, 0, 0, 0)),
                pl.BlockSpec((1, cb), lambda i, j: (0, j))]
    args = [proj, bblk, cblk, consts, d_skip.reshape(1, width)]
    scratch = [pltpu.VMEM((tc, sw), F32)]
    if per_row:
        assert r == tc
        in_specs.append(pl.BlockSpec((tc, sw), lambda i, j: (0, j)))
        args.append(h0_rows)
        hs_shape = jax.ShapeDtypeStruct((r, nb * sw), F32)
        hs_spec = pl.BlockSpec((tc, sw), lambda i, j: (0, j))
    else:
        hs_shape = jax.ShapeDtypeStruct((r // tc, nb, SUBLANES, sw), F32)
        hs_spec = pl.BlockSpec((1, 1, SUBLANES, sw), lambda i, j: (i, j, 0, 0))
        scratch.append(pltpu.VMEM((nb, 1, sw), F32))
    return pl.pallas_call(
        functools.partial(_ssm_kernel, chunks_per_seq=cps, per_row_h0=per_row),
        grid=(r // tc, nb), in_specs=in_specs,
        out_specs=[pl.BlockSpec((tc, cb), lambda i, j: (i, j)), hs_spec],
        out_shape=[jax.ShapeDtypeStruct((r, width), F32), hs_shape],
        scratch_shapes=scratch,
        compiler_params=_cparams(("arbitrary", "arbitrary")), name="ssm")(*args)


def _mix_kernel(oa_ref, gg_ref, ga_ref, gs_ref, x_ref, wglu_ref, bglu_ref, wba_ref, wbs_ref, wout_ref, g_ref, x1_ref):
    gg = gg_ref[...]
    o_ssm = gg * _sigmoid(_dot(gg.astype(BF16), wglu_ref[...]) + bglu_ref[...])
    merged = (_sigmoid(ga_ref[...]) * _dot(oa_ref[...].astype(BF16), wba_ref[...])
              + _sigmoid(gs_ref[...]) * _dot(o_ssm.astype(BF16), wbs_ref[...]))
    mo = _dot(merged.astype(BF16), wout_ref[...])
    x1_ref[...] = x_ref[...] + _rms(mo) * g_ref[...]


def _mix(o_attn, gg, proj, gate_col, x, wglu, bglu, wba, wbs, wout, g, tm):
    r, d = x.shape
    w = o_attn.shape[1]
    gblk = gate_col // d
    assert gblk * d == gate_col
    row = lambda n, c=0: pl.BlockSpec((tm, n), lambda i: (i, c))
    const = lambda a, b: pl.BlockSpec((a, b), lambda i: (0, 0), pipeline_mode=pl.Buffered(1))
    return pl.pallas_call(
        _mix_kernel, grid=(r // tm,),
        in_specs=[row(w), row(w), row(d, gblk), row(d, gblk + 1), row(d), const(w, w), const(1, w), const(w, d),
                  const(w, d), const(d, d), const(1, d)],
        out_specs=row(d), out_shape=jax.ShapeDtypeStruct((r, d), F32),
        compiler_params=_cparams(("arbitrary",)), name="mix")(
            o_attn, gg, proj, proj, x, wglu, bglu.reshape(1, w), wba, wbs, wout, g.reshape(1, d))


def _ffn_kernel(*refs, seq, has_prev, conv_w):
    if has_prev:
        (x_ref, g3_ref, wg_ref, wv_ref, cw_ref, cb_ref, wd_ref, g4_ref, p1_ref, p2_ref,
         y_ref, cs_ref, xn_ref, acc_ref, gbuf_ref, carry_ref) = refs
    else:
        (x_ref, g3_ref, wg_ref, wv_ref, cw_ref, cb_ref, wd_ref, g4_ref,
         y_ref, cs_ref, xn_ref, acc_ref, gbuf_ref, carry_ref) = refs
    i, j = pl.program_id(0), pl.program_id(1)
    tm = x_ref.shape[0]
    assert conv_w == 3

    @pl.when(j == 0)
    def _():
        xn_ref[...] = (_rms(x_ref[...]) * g3_ref[...]).astype(BF16)
        acc_ref[...] = jnp.zeros_like(acc_ref)

    @pl.when(i == 0)
    def _():
        carry_ref[j] = jnp.zeros(carry_ref.shape[1:], F32)

    xn = xn_ref[...]
    t = (i * tm + lax.broadcasted_iota(jnp.int32, (tm, 1), 0)) % seq
    keep = cs_ref.shape[1]
    tn = wg_ref.shape[1]
    part = min(tn, FFN_COL_PART)
    for c0 in range(0, tn, part):
        cols = slice(c0, c0 + part)
        gate = _dot(xn, wg_ref[:, cols])
        val = _dot(xn, wv_ref[:, cols])
        gbuf_ref[0:SUBLANES, cols] = carry_ref[j, :, cols]
        gbuf_ref[SUBLANES:SUBLANES + tm, cols] = gate
        g1 = gbuf_ref[pl.ds(SUBLANES - 1, tm), cols]
        g2 = gbuf_ref[pl.ds(SUBLANES - 2, tm), cols]
        g1 = jnp.where(t >= 1, g1, p1_ref[:, cols] if has_prev else 0.0)
        g2 = jnp.where(t >= 2, g2, p2_ref[:, cols] if has_prev else 0.0)
        cw = cw_ref[:, cols]
        conv = cb_ref[:, cols] + cw[0:1] * g2 + cw[1:2] * g1 + cw[2:3] * gate
        h = _gelu(conv) * val
        acc_ref[...] += _dot(h.astype(BF16), wd_ref[cols, :])
        carry_ref[j, :, cols] = gate[tm - SUBLANES:tm]
        cs_ref[0, :, cols] = gate[tm - keep:tm]

    @pl.when(j == pl.num_programs(1) - 1)
    def _():
        y_ref[...] = x_ref[...] + _rms(acc_ref[...]) * g4_ref[...]


def _ffn(x1, g3, wg, wv, cw, cb, wd, g4, seq, tm, prev=None):
    r, d = x1.shape
    dffp = wg.shape[1]
    tn = FFN_COL_TILE
    nj = dffp // tn
    has_prev = prev is not None
    keep = tm if has_prev else SUBLANES
    in_specs = [pl.BlockSpec((tm, d), lambda i, j: (i, 0)),
                pl.BlockSpec((1, d), lambda i, j: (0, 0)),
                pl.BlockSpec((d, tn), lambda i, j: (0, j)),
                pl.BlockSpec((d, tn), lambda i, j: (0, j)),
                pl.BlockSpec((3, tn), lambda i, j: (0, j)),
                pl.BlockSpec((1, tn), lambda i, j: (0, j)),
                pl.BlockSpec((tn, d), lambda i, j: (j, 0)),
                pl.BlockSpec((1, d), lambda i, j: (0, 0))]
    args = [x1, g3.reshape(1, d), wg, wv, cw, cb.reshape(1, dffp), wd, g4.reshape(1, d)]
    if has_prev:
        assert r == tm
        in_specs += [pl.BlockSpec((tm, tn), lambda i, j: (0, j))] * 2
        args += list(prev)
    n_cs = r // tm
    return pl.pallas_call(
        functools.partial(_ffn_kernel, seq=seq, has_prev=has_prev, conv_w=cw.shape[0]),
        grid=(r // tm, nj), in_specs=in_specs,
        out_specs=[pl.BlockSpec((tm, d), lambda i, j: (i, 0)),
                   pl.BlockSpec((1, keep, tn), lambda i, j: (i, 0, j))],
        out_shape=[jax.ShapeDtypeStruct((r, d), F32), jax.ShapeDtypeStruct((n_cs, keep, dffp), F32)],
        scratch_shapes=[pltpu.VMEM((tm, d), BF16), pltpu.VMEM((tm, d), F32),
                        pltpu.VMEM((tm + SUBLANES, tn), F32), pltpu.VMEM((nj, SUBLANES, tn), F32)],
        compiler_params=_cparams(("arbitrary", "arbitrary")), name="ffn")(*args)


def _layer(xp, xs, cache_k, cache_v, page_table, h0_re, h0_im, conv_prev, wts):
    (norm_mix_pre, norm_mix_post, norm_ffn_pre, norm_ffn_post, w_in, sb_bias, a_re, a_im, log_dt, b_re, b_im,
     c_re, c_im, d_skip, w_glu, b_glu, w_ba, w_bs, w_out, w_up, conv_w, conv_b, w_down) = wts
    bp, seq, d = xp.shape
    bs, nq, _ = xs.shape
    n_heads = sb_bias.shape[0]
    aw = n_heads * HEAD_DIM
    g, p = a_re.shape
    c = b_re.shape[-1]
    sw_total = g * c
    gb = SSM_BLOCK_CH // c
    nb = g // gb
    dff = conv_w.shape[1]
    dffp = -(-dff // FFN_COL_TILE) * FFN_COL_TILE
    u_col, gate_col = 3 * aw, 3 * aw + sw_total
    assert w_in.shape[1] == gate_col + 2 * d

    w_in_b, w_glu_b, w_ba_b, w_bs_b, w_out_b = (a.astype(BF16) for a in (w_in, w_glu, w_ba, w_bs, w_out))
    padc = lambda a: jnp.pad(a, ((0, 0), (0, dffp - dff)))
    wg_b = padc(w_up[:, :dff]).astype(BF16)
    wv_b = padc(w_up[:, dff:]).astype(BF16)
    wd_b = jnp.pad(w_down, ((0, dffp - dff), (0, 0))).astype(BF16)
    cw_p, cb_p = padc(conv_w), jnp.pad(conv_b, (0, dffp - dff))

    powr, powi, bbr, bbi = _ssm_prep(a_re, a_im, log_dt, b_re, b_im)
    bblk, cblk = _ssm_block_weights(bbr, bbi, c_re, c_im)

    def states(h):
        h = h.reshape(-1, nb, 2, gb, p)
        return h[:, :, 0].reshape(-1, g, p), h[:, :, 1].reshape(-1, g, p)

    rp = bp * seq
    x2 = xp.reshape(rp, d)
    rs = bs * nq
    assert SUBLANES % nq == 0
    xs2 = xs.reshape(rs, d)
    proj = _in_proj(x2, norm_mix_pre, w_in_b, _pick_tile(rp, IN_PROJ_ROW_TILE))
    proj_s = _in_proj(xs2, norm_mix_pre, w_in_b, rs)
    k, v = proj[:, aw:2 * aw], proj[:, 2 * aw:3 * aw]
    qs, ks, vs = proj_s[:, :aw], proj_s[:, aw:2 * aw], proj_s[:, 2 * aw:3 * aw]
    fused = _attn_fused(proj, sb_bias, bp, seq, aw, qs, ks, vs, cache_k, cache_v, page_table, bs, nq)
    if fused is None:
        fused = (_attn_prompt(proj, sb_bias, bp, seq, aw),
                 _attn_sample(qs, ks, vs, sb_bias, cache_k, cache_v, page_table, bs, nq))
    o_attn, o_attn_s = fused
    tc = _pick_tile(seq, 512)
    gg, hs = _ssm(proj, u_col, bblk, cblk, _scan_consts(powr, powi, SUBLANES, gb), d_skip, seq, tc)
    hrp, hip = states(hs.reshape(bp, seq // tc, nb, SUBLANES, -1)[:, -1, :, SUBLANES - 1, :].reshape(bp, -1))
    x1 = _mix(o_attn, gg, proj, gate_col, x2, w_glu_b, b_glu, w_ba_b, w_bs_b, w_out_b, norm_mix_post,
              _pick_tile(rp, 256))
    tmf = _pick_tile(seq, 512)
    yp, csp = _ffn(x1, norm_ffn_pre, wg_b, wv_b, cw_p, cb_p, wd_b, norm_ffn_post, seq, tmf)
    out_p = (yp.reshape(bp, seq, d), k.reshape(bp, seq, n_heads, HEAD_DIM), v.reshape(bp, seq, n_heads, HEAD_DIM),
             hrp, hip, csp.reshape(bp, seq // tmf, SUBLANES, dffp)[:, -1, SUBLANES - (conv_w.shape[0] - 1):, :dff])

    proj, o_attn, k, v = proj_s, o_attn_s, ks, vs
    h0 = jnp.stack([h0_re.reshape(bs, nb, gb * p), h0_im.reshape(bs, nb, gb * p)], axis=2).reshape(bs, nb * 2 * gb * p)
    gg, hs = _ssm(proj, u_col, bblk, cblk, _scan_consts(powr, powi, nq, gb), d_skip, nq, rs,
                  h0_rows=jnp.repeat(h0, nq, axis=0))
    hrs, his = states(hs.reshape(bs, nq, -1)[:, nq - 1])
    x1 = _mix(o_attn, gg, proj, gate_col, xs2, w_glu_b, b_glu, w_ba_b, w_bs_b, w_out_b, norm_mix_post, rs)
    t_row = jnp.arange(rs) % nq
    prev = padc(conv_prev.reshape(bs * (conv_w.shape[0] - 1), dff)).reshape(bs, conv_w.shape[0] - 1, dffp)
    prev_rows = jnp.repeat(prev, nq, axis=0)
    p1 = jnp.where((t_row == 0)[:, None], prev_rows[:, 1], 0.0)
    p2 = jnp.where((t_row == 0)[:, None], prev_rows[:, 0], jnp.where((t_row == 1)[:, None], prev_rows[:, 1], 0.0))
    ys, css = _ffn(x1, norm_ffn_pre, wg_b, wv_b, cw_p, cb_p, wd_b, norm_ffn_post, nq, rs, prev=(p1, p2))
    css = css.reshape(bs, nq, dffp)[:, nq - (conv_w.shape[0] - 1):, :dff]
    out_s = (ys.reshape(bs, nq, d), k.reshape(bs, nq, n_heads, HEAD_DIM), v.reshape(bs, nq, n_heads, HEAD_DIM),
             hrs, his, css)
    return out_p, out_s


def kernel(x_prompt, x_sample, cache_k, cache_v, page_table, state_ssm_re, state_ssm_im, state_ffn_conv, norm_mix_pre, norm_mix_post, norm_ffn_pre, norm_ffn_post, w_in, sb_bias, ssm_a_re, ssm_a_im, ssm_log_dt, ssm_b_re, ssm_b_im, ssm_c_re, ssm_c_im, ssm_d, ssm_w_glu, ssm_b_glu, w_branch_attn, w_branch_ssm, w_out, w_up, conv_w, conv_b, w_down):
    depth = w_in.shape[0]
    xp, xs = x_prompt, x_sample
    acc_p, acc_s = [], []
    for layer in range(depth):
        wts = tuple(a[layer] for a in (
            norm_mix_pre, norm_mix_post, norm_ffn_pre, norm_ffn_post, w_in, sb_bias, ssm_a_re, ssm_a_im, ssm_log_dt,
            ssm_b_re, ssm_b_im, ssm_c_re, ssm_c_im, ssm_d, ssm_w_glu, ssm_b_glu, w_branch_attn, w_branch_ssm, w_out,
            w_up, conv_w, conv_b, w_down))
        out_p, out_s = _layer(xp, xs, cache_k[layer], cache_v[layer], page_table, state_ssm_re[layer],
                              state_ssm_im[layer], state_ffn_conv[layer], wts)
        xp, xs = out_p[0], out_s[0]
        acc_p.append(out_p[1:])
        acc_s.append(out_s[1:])
    stack = lambda acc, n: jnp.stack([a[n] for a in acc])
    kp, vp, hrp, hip, cp = (stack(acc_p, n) for n in range(5))
    ks, vs, hrs, his, cs = (stack(acc_s, n) for n in range(5))
    return (xp, xs, kp, vp, ks, vs, hrp, hip, hrs, his, cp, cs)
```

```python
import functools
import math

import jax
import jax.numpy as jnp
from jax import lax
from jax.experimental import pallas as pl
from jax.experimental.pallas import tpu as pltpu

RMS_EPS = 1e-6
LOG2E = math.log2(math.e)
HEAD_DIM = 64
HEADS_PER_BLOCK = 2
LANES = 128
SUBLANES = 8
SSM_BLOCK_CH = 128
VMEM_LIMIT = 56 * 1024 * 1024
ATTN_Q_TILE = 512
ATTN_K_TILE = 512
ATTN_CUMSUM_BLOCK = 256
SAMPLE_PAGE_GROUP = 16
FFN_COL_TILE = 512
FFN_COL_PART = 512
IN_PROJ_COL_TILE = 1024
IN_PROJ_ROW_TILE = 1024
BF16 = jnp.bfloat16
F32 = jnp.float32


def _cparams(sem):
    return pltpu.CompilerParams(dimension_semantics=sem, vmem_limit_bytes=VMEM_LIMIT)


def _pick_tile(n, target, quantum=SUBLANES):
    best = None
    for t in range(quantum, min(n, target) + 1, quantum):
        if n % t == 0:
            best = t
    assert best is not None, (n, target, quantum)
    return best


def _dot(a, b):
    return jnp.dot(a, b, preferred_element_type=F32)


def _dot_nt(a, b):
    return lax.dot_general(a, b, (((1,), (1,)), ((), ())), preferred_element_type=F32)


def _sigmoid(x):
    return 1.0 / (1.0 + jnp.exp(-x))


def _gelu(x):
    c = math.sqrt(2.0 / math.pi)
    return 0.5 * x * (1.0 + jnp.tanh(x * (c + (c * 0.044715) * (x * x))))


def _rms(x):
    return x * lax.rsqrt(jnp.mean(x * x, axis=-1, keepdims=True) + RMS_EPS)


def _softplus(z):
    return jnp.maximum(z, 0.0) + jnp.log(1.0 + jnp.exp2(jnp.abs(z) * (-LOG2E)))


def _ssm_discretise(ar, ai, ldt):
    dt = jnp.exp(ldt)
    mag = jnp.exp(dt * ar)
    return mag * jnp.cos(dt * ai), mag * jnp.sin(dt * ai)


def _ssm_prep_kernel(ar_ref, ai_ref, ldt_ref, ar3_ref, ai3_ref, ldt3_ref, br_ref, bi_ref,
                     powr_ref, powi_ref, bbr_ref, bbi_ref):
    ar, ai = ar3_ref[...], ai3_ref[...]
    abr, abi = _ssm_discretise(ar, ai, ldt3_ref[...])
    nr, ni = abr - 1.0, abi
    den = ar * ar + ai * ai
    fr, fi = (nr * ar + ni * ai) / den, (ni * ar - nr * ai) / den
    br, bi = br_ref[...], bi_ref[...]
    bbr_ref[...] = fr * br - fi * bi
    bbi_ref[...] = fr * bi + fi * br
    abr, abi = _ssm_discretise(ar_ref[...], ai_ref[...], ldt_ref[...])
    pr, pi = abr, abi
    powr_ref[0], powi_ref[0] = pr, pi
    for k in range(1, SUBLANES):
        pr, pi = pr * abr - pi * abi, pr * abi + pi * abr
        powr_ref[k], powi_ref[k] = pr, pi


def _ssm_prep(a_re, a_im, log_dt, b_re, b_im):
    g, p = a_re.shape
    c = b_re.shape[-1]
    out = (jax.ShapeDtypeStruct((SUBLANES, g, p), F32),) * 2 + (jax.ShapeDtypeStruct((g, c, p), F32),) * 2
    return pl.pallas_call(_ssm_prep_kernel, out_shape=out, name="ssm_prep")(
        a_re, a_im, log_dt.reshape(g, 1), a_re.reshape(g, 1, p), a_im.reshape(g, 1, p), log_dt.reshape(g, 1, 1),
        b_re.transpose(0, 2, 1), b_im.transpose(0, 2, 1))


def _ssm_block_weights(bbr, bbi, c_re, c_im):
    g, c, p = bbr.shape
    gb = SSM_BLOCK_CH // c
    nb = g // gb
    eye = jnp.eye(gb, dtype=F32)
    bb = jnp.stack([bbr, bbi], axis=0).reshape(2, nb, gb, c, p)
    bblk = jnp.einsum("rjgcp,gh->jgcrhp", bb, eye).reshape(nb, gb * c, 2 * gb * p)
    cc = jnp.stack([c_re, -c_im], axis=0).reshape(2, nb, gb, c, p)
    cblk = jnp.einsum("rjgcp,gh->jrgphc", cc, eye).reshape(nb, 2 * gb * p, gb * c)
    return bblk.astype(BF16), cblk.astype(BF16)


def _scan_consts(powr, powi, seg, gb):
    _, g, p = powr.shape
    nb = g // gb
    rr = jnp.arange(SUBLANES) % seg
    kinds = []
    for d in (1, 2, 4):
        m = (rr >= d).astype(F32)[:, None, None]
        kinds += [m * powr[d - 1][None], m * powi[d - 1][None]]
    kinds += [powr[rr], powi[rr]]
    k = jnp.stack(kinds, axis=0)
    return k.reshape(8, SUBLANES, nb, gb * p).transpose(2, 0, 1, 3)


def _inproj_kernel(x_ref, g_ref, w_ref, o_ref, xn_ref):
    @pl.when(pl.program_id(1) == 0)
    def _():
        xn_ref[...] = (_rms(x_ref[...]) * g_ref[...]).astype(BF16)

    o_ref[...] = _dot(xn_ref[...], w_ref[...])


def _in_proj(x, g, w_bf16, tm):
    r, d = x.shape
    n = w_bf16.shape[1]
    tn = _pick_tile(n, IN_PROJ_COL_TILE, LANES)
    return pl.pallas_call(
        _inproj_kernel, grid=(r // tm, n // tn),
        in_specs=[pl.BlockSpec((tm, d), lambda i, j: (i, 0)),
                  pl.BlockSpec((1, d), lambda i, j: (0, 0)),
                  pl.BlockSpec((d, tn), lambda i, j: (0, j))],
        out_specs=pl.BlockSpec((None, tm, tn), lambda i, j: (j, i, 0)),
        out_shape=jax.ShapeDtypeStruct((n // tn, r, tn), F32),
        scratch_shapes=[pltpu.VMEM((tm, d), BF16)],
        compiler_params=_cparams(("arbitrary", "arbitrary")), name="in_proj")(x, g.reshape(1, d), w_bf16)


def _proj_spec(proj, rows, blk, row_col, **kw):
    tn = proj.shape[2]
    assert tn % blk == 0

    def index(*ids):
        rb, cb = row_col(*ids)
        return (cb * blk) // tn, rb, ((cb * blk) % tn) // blk

    return pl.BlockSpec((None, rows, blk), index, **kw)


def _sb_weights(z, mask, tail, tri2, latest_first):
    nk = tri2.shape[1]
    n = z.shape[1] // nk
    sp = _softplus(z)
    if mask is not None:
        sp = jnp.where(mask, sp, 0.0)
    hi = sp.astype(BF16)
    lo = (sp - hi.astype(F32)).astype(BF16)
    parts = [None] * n
    for c in (range(n) if latest_first else reversed(range(n))):
        cols = slice(c * nk, (c + 1) * nk)
        local = _dot(jnp.concatenate([hi[:, cols], lo[:, cols]], axis=1), tri2)
        parts[c] = local + tail
        tail = tail + local[:, 0:1]
    rsum = parts[0] if n == 1 else jnp.concatenate(parts, axis=1)
    w = jnp.exp(z - rsum)
    if mask is not None:
        w = jnp.where(mask, w, 0.0)
    return w.astype(BF16), tail


def _tri2(n):
    tri = (jnp.arange(n)[:, None] >= jnp.arange(n)[None, :]).astype(BF16)
    return jnp.concatenate([tri, tri], axis=0)


def _prompt_attn_step(hp, qi, half, bias_ref, q_ref, k_ref, v_ref, tri_ref, o_ref, kb_ref, vb_ref, acc_ref, znext_ref,
                      tail_ref, *, tq, tk, side_work=None):
    assert tq % tk == 0 or tk % tq == 0
    nd = max(tq // tk, 1)
    first_diag = (qi * tq) // tk

    lane = lax.broadcasted_iota(jnp.int32, (1, LANES), 1)
    first = lane < HEAD_DIM
    q = q_ref[...] * (HEAD_DIM ** -0.5)
    qh = (jnp.where(first, q, 0.0).astype(BF16), jnp.where(first, 0.0, q).astype(BF16))
    bias = (bias_ref[HEADS_PER_BLOCK * hp], bias_ref[HEADS_PER_BLOCK * hp + 1])
    tri2 = tri_ref[...]
    def logits(j, h):
        start = pl.multiple_of(j * tk, tk)
        return _dot_nt(qh[h], kb_ref[pl.ds(start, tk), :]) + bias[h]

    def weights(z, j, tail, masked):
        mask = None
        if masked:
            qpos = qi * tq + lax.broadcasted_iota(jnp.int32, (tq, tk), 0)
            mask = j * tk + lax.broadcasted_iota(jnp.int32, (tq, tk), 1) < qpos
        return _sb_weights(z, mask, tail, tri2, latest_first=False)

    def times_v(w, j):
        return _dot(w, vb_ref[pl.ds(pl.multiple_of(j * tk, tk), tk), :])

    assert HEADS_PER_BLOCK == 2

    def body(it, tails):
        j = first_diag - 1 - it
        z1 = znext_ref[...]
        z0 = logits(j, 0)
        w1, t1 = weights(z1, j, tails[1], False)
        acc_ref[1] += times_v(w1, j)
        w0, t0 = weights(z0, j, tails[0], False)
        acc_ref[0] += times_v(w0, j)
        znext_ref[...] = logits(jnp.maximum(j - 1, 0), 1)
        return (t0, t1)

    def below_diagonal(lo, hi, tails):
        return lax.fori_loop(lo, hi, body, tails)

    def start(n_off):
        @pl.when(qi == 0)
        def _():
            kb_ref[...] = k_ref[...].astype(BF16)
            vb_ref[...] = v_ref[...].astype(BF16)

        if side_work is not None:
            side_work()
        tails = [jnp.zeros((tq, 1), F32)] * HEADS_PER_BLOCK
        for dd in reversed(range(nd)):
            for h in range(HEADS_PER_BLOCK):
                j = first_diag + dd
                w, tails[h] = weights(logits(j, h), j, tails[h], True)
                if dd == nd - 1:
                    acc_ref[h] = times_v(w, j)
                else:
                    acc_ref[h] += times_v(w, j)
        znext_ref[...] = logits(jnp.maximum(first_diag - 1, 0), 1)
        return below_diagonal(0, n_off, tuple(tails))

    def finish(lo, tails):
        below_diagonal(lo, first_diag, tails)
        o_ref[...] = jnp.where(first, acc_ref[0], acc_ref[1])

    if half is None:
        finish(first_diag, start(first_diag))
    else:
        split = first_diag // 2

        @pl.when(half == 0)
        def _():
            tails = start(split)
            for h in range(HEADS_PER_BLOCK):
                tail_ref[h] = tails[h]

        @pl.when(half == 1)
        def _():
            if side_work is not None:
                side_work()
            finish(split, tuple(tail_ref[h] for h in range(HEADS_PER_BLOCK)))


def _attn_prompt_kernel(bias_ref, q_ref, k_ref, v_ref, tri_ref, o_ref, kb_ref, vb_ref, acc_ref, znext_ref, *, tq, tk):
    _prompt_attn_step(pl.program_id(1), pl.program_id(2), None, bias_ref, q_ref, k_ref, v_ref, tri_ref, o_ref,
                      kb_ref, vb_ref, acc_ref, znext_ref, None, tq=tq, tk=tk)


def _attn_prompt(proj, sb_bias, bsz, seq, width):
    r = proj.shape[1]
    tk = _pick_tile(seq, ATTN_K_TILE, LANES)
    tq = _pick_tile(seq, ATTN_Q_TILE, tk)
    tc = _pick_tile(tk, ATTN_CUMSUM_BLOCK, LANES)
    nq = seq // tq
    nhp = width // LANES
    return pl.pallas_call(
        functools.partial(_attn_prompt_kernel, tq=tq, tk=tk),
        grid=(bsz, nhp, nq),
        in_specs=[pl.BlockSpec(memory_space=pltpu.SMEM),
                  _proj_spec(proj, tq, LANES, lambda b, hp, qi: (b * nq + qi, hp)),
                  _proj_spec(proj, seq, LANES, lambda b, hp, qi: (b, nhp + hp)),
                  _proj_spec(proj, seq, LANES, lambda b, hp, qi: (b, 2 * nhp + hp)),
                  pl.BlockSpec((2 * tc, tc), lambda b, hp, qi: (0, 0))],
        out_specs=pl.BlockSpec((tq, LANES), lambda b, hp, qi: (b * nq + qi, hp)),
        out_shape=jax.ShapeDtypeStruct((r, width), F32),
        scratch_shapes=[pltpu.VMEM((seq, LANES), BF16), pltpu.VMEM((seq, LANES), BF16),
                        pltpu.VMEM((HEADS_PER_BLOCK, tq, LANES), F32), pltpu.VMEM((tq, tk), F32)],
        compiler_params=_cparams(("arbitrary", "arbitrary", "arbitrary")), name="attn_prompt")(
            sb_bias, proj, proj, proj, _tri2(tc))


def _sample_attn_parts(q_ref, kn_ref, vn_ref, bias_ref, tri_ref, k_refs, v_refs,
                       o_ref, wq_ref, acc_ref, tail_ref, pad_ref, *, n_heads, n_q):
    nrow = n_q * n_heads
    width = n_heads * HEAD_DIM
    assert n_heads & (n_heads - 1) == 0 and HEAD_DIM & (HEAD_DIM - 1) == 0
    head_shift, dim_shift = n_heads.bit_length() - 1, HEAD_DIM.bit_length() - 1
    r_head = lax.broadcasted_iota(jnp.int32, (nrow, width), 0) & (n_heads - 1)
    c_head = lax.broadcasted_iota(jnp.int32, (nrow, width), 1) >> dim_shift
    own_head = r_head == c_head
    page = tri_ref.shape[1]
    tri2 = tri_ref[...]
    bias = bias_ref[...]

    def first():
        q = q_ref[0] * (HEAD_DIM ** -0.5)
        rows = jnp.concatenate([jnp.broadcast_to(q[i:i + 1], (n_heads, width)) for i in range(n_q)], axis=0)
        wq_ref[...] = jnp.where(own_head, rows, 0.0).astype(BF16)
        pad_ref[...] = jnp.zeros_like(pad_ref)
        pad_ref[0:SUBLANES, :] = kn_ref[0]
        z = _dot_nt(wq_ref[...], pad_ref[...].astype(BF16)) + bias
        pad_ref[0:SUBLANES, :] = vn_ref[0]
        kpos = lax.broadcasted_iota(jnp.int32, (nrow, page), 1)
        qpos = lax.broadcasted_iota(jnp.int32, (nrow, page), 0) >> head_shift
        w, tl = _sb_weights(z, kpos < qpos, jnp.zeros((nrow, 1), F32), tri2, latest_first=True)
        acc_ref[...] = _dot(w, pad_ref[...].astype(BF16))
        tail_ref[...] = tl

    def pages():
        tail = tail_ref[...]
        acc = acc_ref[...]
        for g0 in range(0, len(k_refs), SAMPLE_PAGE_GROUP):
            kcat = jnp.concatenate([r[0].astype(BF16) for r in k_refs[g0:g0 + SAMPLE_PAGE_GROUP]], axis=1)
            vcat = jnp.concatenate([r[0].astype(BF16) for r in v_refs[g0:g0 + SAMPLE_PAGE_GROUP]], axis=1)
            z = _dot(wq_ref[...], kcat) + bias
            w, tail = _sb_weights(z, None, tail, tri2, latest_first=True)
            acc = acc + _dot_nt(w, vcat)
        acc_ref[...] = acc
        tail_ref[...] = tail

    def last():
        own = jnp.where(own_head, acc_ref[...], 0.0)
        o_ref[0] = jnp.concatenate(
            [jnp.sum(own[i * n_heads:(i + 1) * n_heads], axis=0, keepdims=True) for i in range(n_q)], axis=0)

    return first, pages, last


def _attn_sample_kernel(pt_ref, q_ref, kn_ref, vn_ref, bias_ref, tri_ref, *rest, n_heads, n_q, pages_per_step):
    del pt_ref
    k_refs = rest[:pages_per_step]
    v_refs = rest[pages_per_step:2 * pages_per_step]
    s = pl.program_id(1)
    first, pages, last = _sample_attn_parts(q_ref, kn_ref, vn_ref, bias_ref, tri_ref, k_refs, v_refs,
                                            *rest[2 * pages_per_step:], n_heads=n_heads, n_q=n_q)
    pl.when(s == 0)(first)
    pages()
    pl.when(s == pl.num_programs(1) - 1)(last)


def _sample_attn_operands(q, k_new, v_new, sb_bias, cache_k, cache_v, bs, n_q, pps, seq_and_group):
    n_pool, page, n_heads, hd = cache_k.shape
    width = n_heads * hd
    nrow = n_q * n_heads
    ck = cache_k.transpose(0, 2, 3, 1).reshape(n_pool, width, page)
    cv = cache_v.transpose(0, 2, 3, 1).reshape(n_pool, width, page)
    pad = lambda a: jnp.pad(a.reshape(bs, n_q, width), ((0, 0), (0, SUBLANES - n_q), (0, 0)))
    bias_col = jnp.tile(sb_bias, n_q).reshape(nrow, 1)

    def per_seq(rows):
        return pl.BlockSpec((1, rows, width), lambda *ids: (seq_and_group(*ids[:-1])[0], 0, 0))

    def page_spec(p):
        def index(*ids):
            pt = ids[-1]
            sb, grp = seq_and_group(*ids[:-1])
            return pt[sb, pt.shape[1] - 1 - (grp * pps + p)], 0, 0
        return pl.BlockSpec((1, width, page), index)

    const = lambda shape: pl.BlockSpec(shape, lambda *ids: (0, 0))
    args = [q.reshape(bs, n_q, width), pad(k_new), pad(v_new), bias_col, _tri2(page)] + [ck] * pps + [cv] * pps
    in_specs = ([per_seq(n_q), per_seq(SUBLANES), per_seq(SUBLANES), const((nrow, 1)), const((2 * page, page))]
                + [page_spec(p) for p in range(pps)] * 2)
    scratch = [pltpu.VMEM((nrow, width), BF16), pltpu.VMEM((nrow, width), F32),
               pltpu.VMEM((nrow, 1), F32), pltpu.VMEM((page, width), F32)]
    return args, in_specs, per_seq(n_q), jax.ShapeDtypeStruct((bs, n_q, width), F32), scratch


def _attn_sample(q, k_new, v_new, sb_bias, cache_k, cache_v, page_table, bs, n_q):
    n_heads = cache_k.shape[2]
    n_pages = page_table.shape[1]
    pps = _pick_tile(n_pages, SAMPLE_PAGE_GROUP, 1)
    args, in_specs, out_spec, out_shape, scratch = _sample_attn_operands(
        q, k_new, v_new, sb_bias, cache_k, cache_v, bs, n_q, pps, lambda b, s: (b, s))
    grid_spec = pltpu.PrefetchScalarGridSpec(
        num_scalar_prefetch=1, grid=(bs, n_pages // pps), in_specs=in_specs, out_specs=out_spec,
        scratch_shapes=scratch)
    out = pl.pallas_call(
        functools.partial(_attn_sample_kernel, n_heads=n_heads, n_q=n_q, pages_per_step=pps),
        grid_spec=grid_spec, out_shape=out_shape,
        compiler_params=_cparams(("arbitrary", "arbitrary")), name="attn_sample")(page_table, *args)
    return out.reshape(bs * n_q, -1)


def _attn_fused_kernel(pt_ref, bias_ref, q_ref, k_ref, v_ref, tri_ref, *rest, tq, tk, n_heads, n_q, pages_per_step,
                       steps_per_seq):
    del pt_ref
    sq_ref, kn_ref, vn_ref, sbias_ref, stri_ref = rest[:5]
    k_refs = rest[5:5 + pages_per_step]
    v_refs = rest[5 + pages_per_step:5 + 2 * pages_per_step]
    (o_ref, os_ref, kb_ref, vb_ref, acc_ref, znext_ref, ptail_ref,
     wq_ref, sacc_ref, tail_ref, pad_ref) = rest[5 + 2 * pages_per_step:]
    hp, qi, half = pl.program_id(1), pl.program_id(2), pl.program_id(3)
    lin = ((pl.program_id(0) * pl.num_programs(1) + hp) * pl.num_programs(2) + qi) * pl.num_programs(3) + half
    grp = lin % steps_per_seq
    first, pages, last = _sample_attn_parts(sq_ref, kn_ref, vn_ref, sbias_ref, stri_ref, k_refs, v_refs,
                                            os_ref, wq_ref, sacc_ref, tail_ref, pad_ref, n_heads=n_heads, n_q=n_q)
    pl.when(grp == 0)(first)
    _prompt_attn_step(hp, qi, half, bias_ref, q_ref, k_ref, v_ref, tri_ref, o_ref, kb_ref, vb_ref, acc_ref, znext_ref,
                      ptail_ref, tq=tq, tk=tk, side_work=pages)
    pl.when(grp == steps_per_seq - 1)(last)


def _attn_fused(proj, sb_bias, bsz, seq, width, q_s, k_new, v_new, cache_k, cache_v, page_table, bs, n_q):
    r = proj.shape[1]
    n_heads = cache_k.shape[2]
    n_pages = page_table.shape[1]
    tk = _pick_tile(seq, ATTN_K_TILE, LANES)
    tq = _pick_tile(seq, ATTN_Q_TILE, tk)
    tc = _pick_tile(tk, ATTN_CUMSUM_BLOCK, LANES)
    nq = seq // tq
    nhp = width // LANES
    halves = 2
    n_steps = bsz * nhp * nq * halves
    pps = (bs * n_pages) // n_steps
    if pps == 0 or pps * n_steps != bs * n_pages or n_pages % pps or pps > 2 * SAMPLE_PAGE_GROUP:
        return None
    sps = n_pages // pps

    def seq_and_group(b, hp, qi, half):
        lin = ((b * nhp + hp) * nq + qi) * halves + half
        return lin // sps, lin % sps

    s_args, s_specs, s_out_spec, s_out_shape, s_scratch = _sample_attn_operands(
        q_s, k_new, v_new, sb_bias, cache_k, cache_v, bs, n_q, pps, seq_and_group)
    whole_seq = lambda c0: _proj_spec(proj, seq, LANES, lambda b, hp, qi, half, pt: (b, c0 + hp),
                                      pipeline_mode=pl.Buffered(1))
    q_index = lambda b, hp, qi, half, pt: (b * nq + qi, hp)
    grid_spec = pltpu.PrefetchScalarGridSpec(
        num_scalar_prefetch=1, grid=(bsz, nhp, nq, halves),
        in_specs=[pl.BlockSpec(memory_space=pltpu.SMEM), _proj_spec(proj, tq, LANES, q_index),
                  whole_seq(nhp), whole_seq(2 * nhp),
                  pl.BlockSpec((2 * tc, tc), lambda b, hp, qi, half, pt: (0, 0))] + s_specs,
        out_specs=[pl.BlockSpec((tq, LANES), q_index), s_out_spec],
        scratch_shapes=[pltpu.VMEM((seq, LANES), BF16), pltpu.VMEM((seq, LANES), BF16),
                        pltpu.VMEM((HEADS_PER_BLOCK, tq, LANES), F32), pltpu.VMEM((tq, tk), F32),
                        pltpu.VMEM((HEADS_PER_BLOCK, tq, 1), F32)] + s_scratch)
    o_p, o_s = pl.pallas_call(
        functools.partial(_attn_fused_kernel, tq=tq, tk=tk, n_heads=n_heads, n_q=n_q, pages_per_step=pps,
                          steps_per_seq=sps),
        grid_spec=grid_spec, out_shape=[jax.ShapeDtypeStruct((r, width), F32), s_out_shape],
        compiler_params=_cparams(("arbitrary",) * 4), name="attn_fused")(
            page_table, sb_bias, proj, proj, proj, _tri2(tc), *s_args)
    return o_p, o_s.reshape(bs * n_q, -1)


def _ssm_kernel(*refs, chunks_per_seq, per_row_h0):
    if per_row_h0:
        u_ref, bw_ref, cw_ref, k_ref, d_ref, h0_ref, gg_ref, hs_ref, bu_ref = refs
    else:
        u_ref, bw_ref, cw_ref, k_ref, d_ref, gg_ref, hs_ref, bu_ref, carry_ref = refs
    i, j = pl.program_id(0), pl.program_id(1)
    rows = u_ref.shape[0]
    half = bu_ref.shape[1] // 2
    u = u_ref[...]
    bu_ref[...] = _dot(u.astype(BF16), bw_ref[0])

    if not per_row_h0:
        @pl.when(i % chunks_per_seq == 0)
        def _():
            carry_ref[j] = jnp.zeros((1, 2 * half), F32)

    def tile(tix, carry):
        r0 = pl.multiple_of(tix * SUBLANES, SUBLANES)
        br = bu_ref[pl.ds(r0, SUBLANES), 0:half]
        bi = bu_ref[pl.ds(r0, SUBLANES), half:2 * half]
        for n, d in enumerate((1, 2, 4)):
            ar, ai = k_ref[0, 2 * n], k_ref[0, 2 * n + 1]
            sr, si = pltpu.roll(br, d, 0), pltpu.roll(bi, d, 0)
            br, bi = br + ar * sr - ai * si, bi + ar * si + ai * sr
        pr, pi = k_ref[0, 6], k_ref[0, 7]
        if per_row_h0:
            cr = h0_ref[pl.ds(r0, SUBLANES), 0:half]
            ci = h0_ref[pl.ds(r0, SUBLANES), half:2 * half]
        else:
            cr, ci = carry
        hr = br + pr * cr - pi * ci
        hi = bi + pr * ci + pi * cr
        bu_ref[pl.ds(r0, SUBLANES), 0:half] = hr
        bu_ref[pl.ds(r0, SUBLANES), half:2 * half] = hi
        return (hr[SUBLANES - 1:SUBLANES], hi[SUBLANES - 1:SUBLANES])

    if per_row_h0:
        init = (jnp.zeros((1, half), F32),) * 2
    else:
        init = (carry_ref[j, :, 0:half], carry_ref[j, :, half:2 * half])
    last = lax.fori_loop(0, rows // SUBLANES, tile, init)
    if per_row_h0:
        hs_ref[...] = bu_ref[...]
    else:
        carry_ref[j, :, 0:half] = last[0]
        carry_ref[j, :, half:2 * half] = last[1]
        hs_ref[0, 0] = bu_ref[rows - SUBLANES:rows, :]
    y = _dot(bu_ref[...].astype(BF16), cw_ref[0]) + d_ref[...] * u
    gg_ref[...] = _gelu(y)


def _ssm(proj, u_col, bblk, cblk, consts, d_skip, seq, tc, h0_rows=None):
    r = proj.shape[1]
    width = d_skip.shape[0]
    nb, cb, sw = bblk.shape
    per_row = h0_rows is not None
    cps = max(seq // tc, 1)
    u_blk = u_col // cb
    assert u_blk * cb == u_col
    in_specs = [_proj_spec(proj, tc, cb, lambda i, j: (i, u_blk + j)),
                pl.BlockSpec((1, cb, sw), lambda i, j: (j, 0, 0)),
                pl.BlockSpec((1, sw, cb), lambda i, j: (j, 0, 0)),
                pl.BlockSpec((1, 8, SUBLANES, sw // 2), lambda i, j: (j, 0, 0, 0)),
                pl.BlockSpec((1, cb), lambda i, j: (0, j))]
    args = [proj, bblk, cblk, consts, d_skip.reshape(1, width)]
    scratch = [pltpu.VMEM((tc, sw), F32)]
    if per_row:
        assert r == tc
        in_specs.append(pl.BlockSpec((tc, sw), lambda i, j: (0, j)))
        args.append(h0_rows)
        hs_shape = jax.ShapeDtypeStruct((r, nb * sw), F32)
        hs_spec = pl.BlockSpec((tc, sw), lambda i, j: (0, j))
    else:
        hs_shape = jax.ShapeDtypeStruct((r // tc, nb, SUBLANES, sw), F32)
        hs_spec = pl.BlockSpec((1, 1, SUBLANES, sw), lambda i, j: (i, j, 0, 0))
        scratch.append(pltpu.VMEM((nb, 1, sw), F32))
    return pl.pallas_call(
        functools.partial(_ssm_kernel, chunks_per_seq=cps, per_row_h0=per_row),
        grid=(r // tc, nb), in_specs=in_specs,
        out_specs=[pl.BlockSpec((tc, cb), lambda i, j: (i, j)), hs_spec],
        out_shape=[jax.ShapeDtypeStruct((r, width), F32), hs_shape],
        scratch_shapes=scratch,
        compiler_params=_cparams(("arbitrary", "arbitrary")), name="ssm")(*args)


def _mix_kernel(oa_ref, gg_ref, x_ref, wglu_ref, bglu_ref, wba_ref, wbs_ref, wout_ref, g_ref, *rest):
    gate_refs, x1_ref = rest[:-1], rest[-1]
    half = len(gate_refs) // 2
    ga = jnp.concatenate([ref[...] for ref in gate_refs[:half]], axis=1)
    gs = jnp.concatenate([ref[...] for ref in gate_refs[half:]], axis=1)
    gg = gg_ref[...]
    o_ssm = gg * _sigmoid(_dot(gg.astype(BF16), wglu_ref[...]) + bglu_ref[...])
    merged = (_sigmoid(ga) * _dot(oa_ref[...].astype(BF16), wba_ref[...])
              + _sigmoid(gs) * _dot(o_ssm.astype(BF16), wbs_ref[...]))
    mo = _dot(merged.astype(BF16), wout_ref[...])
    x1_ref[...] = x_ref[...] + _rms(mo) * g_ref[...]


def _mix(o_attn, gg, proj, gate_col, x, wglu, bglu, wba, wbs, wout, g, tm):
    r, d = x.shape
    w = o_attn.shape[1]
    tn = proj.shape[2]
    gblk = gate_col // tn
    assert gblk * tn == gate_col and d % tn == 0
    n_gate = 2 * (d // tn)
    row = lambda n: pl.BlockSpec((tm, n), lambda i: (i, 0))
    const = lambda a, b: pl.BlockSpec((a, b), lambda i: (0, 0), pipeline_mode=pl.Buffered(1))
    gate = lambda c: _proj_spec(proj, tm, tn, lambda i: (i, gblk + c))
    return pl.pallas_call(
        _mix_kernel, grid=(r // tm,),
        in_specs=[row(w), row(w), row(d), const(w, w), const(1, w), const(w, d), const(w, d), const(d, d),
                  const(1, d)] + [gate(c) for c in range(n_gate)],
        out_specs=row(d), out_shape=jax.ShapeDtypeStruct((r, d), F32),
        compiler_params=_cparams(("arbitrary",)), name="mix")(
            o_attn, gg, x, wglu, bglu.reshape(1, w), wba, wbs, wout, g.reshape(1, d), *([proj] * n_gate))


def _ffn_kernel(*refs, seq, has_prev, conv_w):
    if has_prev:
        (x_ref, g3_ref, wg_ref, wv_ref, cw_ref, cb_ref, wd_ref, g4_ref, p1_ref, p2_ref,
         y_ref, cs_ref, xn_ref, acc_ref, gbuf_ref, carry_ref) = refs
    else:
        (x_ref, g3_ref, wg_ref, wv_ref, cw_ref, cb_ref, wd_ref, g4_ref,
         y_ref, cs_ref, xn_ref, acc_ref, gbuf_ref, carry_ref) = refs
    i, j = pl.program_id(0), pl.program_id(1)
    tm = x_ref.shape[0]
    assert conv_w == 3

    @pl.when(j == 0)
    def _():
        xn_ref[...] = (_rms(x_ref[...]) * g3_ref[...]).astype(BF16)
        acc_ref[...] = jnp.zeros_like(acc_ref)

    @pl.when(i == 0)
    def _():
        carry_ref[j] = jnp.zeros(carry_ref.shape[1:], F32)

    xn = xn_ref[...]
    t = (i * tm + lax.broadcasted_iota(jnp.int32, (tm, 1), 0)) % seq
    keep = cs_ref.shape[1]
    tn = wg_ref.shape[1]
    part = min(tn, FFN_COL_PART)
    for c0 in range(0, tn, part):
        cols = slice(c0, c0 + part)
        gate = _dot(xn, wg_ref[:, cols])
        val = _dot(xn, wv_ref[:, cols])
        gbuf_ref[0:SUBLANES, cols] = carry_ref[j, :, cols]
        gbuf_ref[SUBLANES:SUBLANES + tm, cols] = gate
        g1 = gbuf_ref[pl.ds(SUBLANES - 1, tm), cols]
        g2 = gbuf_ref[pl.ds(SUBLANES - 2, tm), cols]
        g1 = jnp.where(t >= 1, g1, p1_ref[:, cols] if has_prev else 0.0)
        g2 = jnp.where(t >= 2, g2, p2_ref[:, cols] if has_prev else 0.0)
        cw = cw_ref[:, cols]
        conv = cb_ref[:, cols] + cw[0:1] * g2 + cw[1:2] * g1 + cw[2:3] * gate
        h = _gelu(conv) * val
        acc_ref[...] += _dot(h.astype(BF16), wd_ref[cols, :])
        carry_ref[j, :, cols] = gate[tm - SUBLANES:tm]
        cs_ref[0, :, cols] = gate[tm - keep:tm]

    @pl.when(j == pl.num_programs(1) - 1)
    def _():
        y_ref[...] = x_ref[...] + _rms(acc_ref[...]) * g4_ref[...]


def _ffn(x1, g3, wg, wv, cw, cb, wd, g4, seq, tm, prev=None):
    r, d = x1.shape
    dffp = wg.shape[1]
    tn = FFN_COL_TILE
    nj = dffp // tn
    has_prev = prev is not None
    keep = tm if has_prev else SUBLANES
    in_specs = [pl.BlockSpec((tm, d), lambda i, j: (i, 0)),
                pl.BlockSpec((1, d), lambda i, j: (0, 0)),
                pl.BlockSpec((d, tn), lambda i, j: (0, j)),
                pl.BlockSpec((d, tn), lambda i, j: (0, j)),
                pl.BlockSpec((3, tn), lambda i, j: (0, j)),
                pl.BlockSpec((1, tn), lambda i, j: (0, j)),
                pl.BlockSpec((tn, d), lambda i, j: (j, 0)),
                pl.BlockSpec((1, d), lambda i, j: (0, 0))]
    args = [x1, g3.reshape(1, d), wg, wv, cw, cb.reshape(1, dffp), wd, g4.reshape(1, d)]
    if has_prev:
        assert r == tm
        in_specs += [pl.BlockSpec((tm, tn), lambda i, j: (0, j))] * 2
        args += list(prev)
    n_cs = r // tm
    return pl.pallas_call(
        functools.partial(_ffn_kernel, seq=seq, has_prev=has_prev, conv_w=cw.shape[0]),
        grid=(r // tm, nj), in_specs=in_specs,
        out_specs=[pl.BlockSpec((tm, d), lambda i, j: (i, 0)),
                   pl.BlockSpec((1, keep, tn), lambda i, j: (i, 0, j))],
        out_shape=[jax.ShapeDtypeStruct((r, d), F32), jax.ShapeDtypeStruct((n_cs, keep, dffp), F32)],
        scratch_shapes=[pltpu.VMEM((tm, d), BF16), pltpu.VMEM((tm, d), F32),
                        pltpu.VMEM((tm + SUBLANES, tn), F32), pltpu.VMEM((nj, SUBLANES, tn), F32)],
        compiler_params=_cparams(("arbitrary", "arbitrary")), name="ffn")(*args)


def _layer(xp, xs, cache_k, cache_v, page_table, h0_re, h0_im, conv_prev, wts):
    (norm_mix_pre, norm_mix_post, norm_ffn_pre, norm_ffn_post, w_in, sb_bias, a_re, a_im, log_dt, b_re, b_im,
     c_re, c_im, d_skip, w_glu, b_glu, w_ba, w_bs, w_out, w_up, conv_w, conv_b, w_down) = wts
    bp, seq, d = xp.shape
    bs, nq, _ = xs.shape
    n_heads = sb_bias.shape[0]
    aw = n_heads * HEAD_DIM
    g, p = a_re.shape
    c = b_re.shape[-1]
    sw_total = g * c
    gb = SSM_BLOCK_CH // c
    nb = g // gb
    dff = conv_w.shape[1]
    dffp = -(-dff // FFN_COL_TILE) * FFN_COL_TILE
    u_col, gate_col = 3 * aw, 3 * aw + sw_total
    assert w_in.shape[1] == gate_col + 2 * d

    w_in_b, w_glu_b, w_ba_b, w_bs_b, w_out_b = (a.astype(BF16) for a in (w_in, w_glu, w_ba, w_bs, w_out))
    padc = lambda a: jnp.pad(a, ((0, 0), (0, dffp - dff)))
    wg_b = padc(w_up[:, :dff]).astype(BF16)
    wv_b = padc(w_up[:, dff:]).astype(BF16)
    wd_b = jnp.pad(w_down, ((0, dffp - dff), (0, 0))).astype(BF16)
    cw_p, cb_p = padc(conv_w), jnp.pad(conv_b, (0, dffp - dff))

    powr, powi, bbr, bbi = _ssm_prep(a_re, a_im, log_dt, b_re, b_im)
    bblk, cblk = _ssm_block_weights(bbr, bbi, c_re, c_im)

    def states(h):
        h = h.reshape(-1, nb, 2, gb, p)
        return h[:, :, 0].reshape(-1, g, p), h[:, :, 1].reshape(-1, g, p)

    rp = bp * seq
    x2 = xp.reshape(rp, d)
    rs = bs * nq
    assert SUBLANES % nq == 0
    xs2 = xs.reshape(rs, d)
    proj = _in_proj(x2, norm_mix_pre, w_in_b, _pick_tile(rp, IN_PROJ_ROW_TILE))
    proj_s = _in_proj(xs2, norm_mix_pre, w_in_b, rs)
    assert proj.shape[2] == aw
    k, v = proj[1], proj[2]
    qs, ks, vs = proj_s[0], proj_s[1], proj_s[2]
    fused = _attn_fused(proj, sb_bias, bp, seq, aw, qs, ks, vs, cache_k, cache_v, page_table, bs, nq)
    if fused is None:
        fused = (_attn_prompt(proj, sb_bias, bp, seq, aw),
                 _attn_sample(qs, ks, vs, sb_bias, cache_k, cache_v, page_table, bs, nq))
    o_attn, o_attn_s = fused
    tc = _pick_tile(seq, 512)
    gg, hs = _ssm(proj, u_col, bblk, cblk, _scan_consts(powr, powi, SUBLANES, gb), d_skip, seq, tc)
    hrp, hip = states(hs.reshape(bp, seq // tc, nb, SUBLANES, -1)[:, -1, :, SUBLANES - 1, :].reshape(bp, -1))
    x1 = _mix(o_attn, gg, proj, gate_col, x2, w_glu_b, b_glu, w_ba_b, w_bs_b, w_out_b, norm_mix_post,
              _pick_tile(rp, 256))
    tmf = _pick_tile(seq, 512)
    yp, csp = _ffn(x1, norm_ffn_pre, wg_b, wv_b, cw_p, cb_p, wd_b, norm_ffn_post, seq, tmf)
    out_p = (yp.reshape(bp, seq, d), k.reshape(bp, seq, n_heads, HEAD_DIM), v.reshape(bp, seq, n_heads, HEAD_DIM),
             hrp, hip, csp.reshape(bp, seq // tmf, SUBLANES, dffp)[:, -1, SUBLANES - (conv_w.shape[0] - 1):, :dff])

    proj, o_attn, k, v = proj_s, o_attn_s, ks, vs
    h0 = jnp.stack([h0_re.reshape(bs, nb, gb * p), h0_im.reshape(bs, nb, gb * p)], axis=2).reshape(bs, nb * 2 * gb * p)
    gg, hs = _ssm(proj, u_col, bblk, cblk, _scan_consts(powr, powi, nq, gb), d_skip, nq, rs,
                  h0_rows=jnp.repeat(h0, nq, axis=0))
    hrs, his = states(hs.reshape(bs, nq, -1)[:, nq - 1])
    x1 = _mix(o_attn, gg, proj, gate_col, xs2, w_glu_b, b_glu, w_ba_b, w_bs_b, w_out_b, norm_mix_post, rs)
    t_row = jnp.arange(rs) % nq
    prev = padc(conv_prev.reshape(bs * (conv_w.shape[0] - 1), dff)).reshape(bs, conv_w.shape[0] - 1, dffp)
    prev_rows = jnp.repeat(prev, nq, axis=0)
    p1 = jnp.where((t_row == 0)[:, None], prev_rows[:, 1], 0.0)
    p2 = jnp.where((t_row == 0)[:, None], prev_rows[:, 0], jnp.where((t_row == 1)[:, None], prev_rows[:, 1], 0.0))
    ys, css = _ffn(x1, norm_ffn_pre, wg_b, wv_b, cw_p, cb_p, wd_b, norm_ffn_post, nq, rs, prev=(p1, p2))
    css = css.reshape(bs, nq, dffp)[:, nq - (conv_w.shape[0] - 1):, :dff]
    out_s = (ys.reshape(bs, nq, d), k.reshape(bs, nq, n_heads, HEAD_DIM), v.reshape(bs, nq, n_heads, HEAD_DIM),
             hrs, his, css)
    return out_p, out_s


def kernel(x_prompt, x_sample, cache_k, cache_v, page_table, state_ssm_re, state_ssm_im, state_ffn_conv, norm_mix_pre, norm_mix_post, norm_ffn_pre, norm_ffn_post, w_in, sb_bias, ssm_a_re, ssm_a_im, ssm_log_dt, ssm_b_re, ssm_b_im, ssm_c_re, ssm_c_im, ssm_d, ssm_w_glu, ssm_b_glu, w_branch_attn, w_branch_ssm, w_out, w_up, conv_w, conv_b, w_down):
    depth = w_in.shape[0]
    xp, xs = x_prompt, x_sample
    acc_p, acc_s = [], []
    for layer in range(depth):
        wts = tuple(a[layer] for a in (
            norm_mix_pre, norm_mix_post, norm_ffn_pre, norm_ffn_post, w_in, sb_bias, ssm_a_re, ssm_a_im, ssm_log_dt,
            ssm_b_re, ssm_b_im, ssm_c_re, ssm_c_im, ssm_d, ssm_w_glu, ssm_b_glu, w_branch_attn, w_branch_ssm, w_out,
            w_up, conv_w, conv_b, w_down))
        out_p, out_s = _layer(xp, xs, cache_k[layer], cache_v[layer], page_table, state_ssm_re[layer],
                              state_ssm_im[layer], state_ffn_conv[layer], wts)
        xp, xs = out_p[0], out_s[0]
        acc_p.append(out_p[1:])
        acc_s.append(out_s[1:])
    stack = lambda acc, n: jnp.stack([a[n] for a in acc])
    kp, vp, hrp, hip, cp = (stack(acc_p, n) for n in range(5))
    ks, vs, hrs, his, cs = (stack(acc_s, n) for n in range(5))
    return (xp, xs, kp, vp, ks, vs, hrp, hip, hrs, his, cp, cs)
```

```python
import functools
import math

import jax
import jax.numpy as jnp
from jax import lax
from jax.experimental import pallas as pl
from jax.experimental.pallas import tpu as pltpu

RMS_EPS = 1e-6
LOG2E = math.log2(math.e)
HEAD_DIM = 64
HEADS_PER_BLOCK = 2
LANES = 128
SUBLANES = 8
SSM_BLOCK_CH = 128
SSM_ROW_CHUNK = 1024
VMEM_LIMIT = 56 * 1024 * 1024
ATTN_Q_TILE = 512
ATTN_K_TILE = 512
ATTN_CUMSUM_BLOCK = 256
SAMPLE_PAGE_GROUP = 16
FFN_COL_TILE = 512
FFN_COL_PART = 512
IN_PROJ_COL_TILE = 1024
IN_PROJ_ROW_TILE = 1024
BF16 = jnp.bfloat16
F32 = jnp.float32


def _cparams(sem):
    return pltpu.CompilerParams(dimension_semantics=sem, vmem_limit_bytes=VMEM_LIMIT)


def _pick_tile(n, target, quantum=SUBLANES):
    best = None
    for t in range(quantum, min(n, target) + 1, quantum):
        if n % t == 0:
            best = t
    assert best is not None, (n, target, quantum)
    return best


def _dot(a, b):
    return jnp.dot(a, b, preferred_element_type=F32)


def _dot_nt(a, b):
    return lax.dot_general(a, b, (((1,), (1,)), ((), ())), preferred_element_type=F32)


def _sigmoid(x):
    return 1.0 / (1.0 + jnp.exp(-x))


def _gelu(x):
    c = math.sqrt(2.0 / math.pi)
    return 0.5 * x * (1.0 + jnp.tanh(x * (c + (c * 0.044715) * (x * x))))


def _rms(x):
    return x * lax.rsqrt(jnp.mean(x * x, axis=-1, keepdims=True) + RMS_EPS)


def _softplus(z):
    return jnp.maximum(z, 0.0) + jnp.log(1.0 + jnp.exp2(jnp.abs(z) * (-LOG2E)))


def _ssm_discretise(ar, ai, ldt):
    dt = jnp.exp(ldt)
    mag = jnp.exp(dt * ar)
    return mag * jnp.cos(dt * ai), mag * jnp.sin(dt * ai)


def _ssm_prep_kernel(ar_ref, ai_ref, ldt_ref, ar3_ref, ai3_ref, ldt3_ref, br_ref, bi_ref,
                     powr_ref, powi_ref, bbr_ref, bbi_ref):
    ar, ai = ar3_ref[...], ai3_ref[...]
    abr, abi = _ssm_discretise(ar, ai, ldt3_ref[...])
    nr, ni = abr - 1.0, abi
    den = ar * ar + ai * ai
    fr, fi = (nr * ar + ni * ai) / den, (ni * ar - nr * ai) / den
    br, bi = br_ref[...], bi_ref[...]
    bbr_ref[...] = fr * br - fi * bi
    bbi_ref[...] = fr * bi + fi * br
    abr, abi = _ssm_discretise(ar_ref[...], ai_ref[...], ldt_ref[...])
    pr, pi = abr, abi
    powr_ref[0], powi_ref[0] = pr, pi
    for k in range(1, SUBLANES):
        pr, pi = pr * abr - pi * abi, pr * abi + pi * abr
        powr_ref[k], powi_ref[k] = pr, pi


def _ssm_prep(a_re, a_im, log_dt, b_re, b_im):
    g, p = a_re.shape
    c = b_re.shape[-1]
    out = (jax.ShapeDtypeStruct((SUBLANES, g, p), F32),) * 2 + (jax.ShapeDtypeStruct((g, c, p), F32),) * 2
    return pl.pallas_call(_ssm_prep_kernel, out_shape=out, name="ssm_prep")(
        a_re, a_im, log_dt.reshape(g, 1), a_re.reshape(g, 1, p), a_im.reshape(g, 1, p), log_dt.reshape(g, 1, 1),
        b_re.transpose(0, 2, 1), b_im.transpose(0, 2, 1))


def _ssm_block_weights(bbr, bbi, c_re, c_im):
    g, c, p = bbr.shape
    gb = SSM_BLOCK_CH // c
    nb = g // gb
    eye = jnp.eye(gb, dtype=F32)
    bb = jnp.stack([bbr, bbi], axis=0).reshape(2, nb, gb, c, p)
    bblk = jnp.einsum("rjgcp,gh->jgcrhp", bb, eye).reshape(nb, gb * c, 2 * gb * p)
    cc = jnp.stack([c_re, -c_im], axis=0).reshape(2, nb, gb, c, p)
    cblk = jnp.einsum("rjgcp,gh->jrgphc", cc, eye).reshape(nb, 2 * gb * p, gb * c)
    return bblk.astype(BF16), cblk.astype(BF16)


def _scan_consts(powr, powi, seg, gb):
    _, g, p = powr.shape
    nb = g // gb
    rr = jnp.arange(SUBLANES) % seg
    kinds = []
    for d in (1, 2, 4):
        m = (rr >= d).astype(F32)[:, None, None]
        kinds += [m * powr[d - 1][None], m * powi[d - 1][None]]
    kinds += [powr[rr], powi[rr]]
    k = jnp.stack(kinds, axis=0)
    return k.reshape(8, SUBLANES, nb, gb * p).transpose(2, 0, 1, 3)


def _inproj_kernel(x_ref, g_ref, w_ref, o_ref, xn_ref):
    @pl.when(pl.program_id(1) == 0)
    def _():
        xn_ref[...] = (_rms(x_ref[...]) * g_ref[...]).astype(BF16)

    o_ref[...] = _dot(xn_ref[...], w_ref[...])


def _in_proj(x, g, w_bf16, tm):
    r, d = x.shape
    n = w_bf16.shape[1]
    tn = _pick_tile(n, IN_PROJ_COL_TILE, LANES)
    return pl.pallas_call(
        _inproj_kernel, grid=(r // tm, n // tn),
        in_specs=[pl.BlockSpec((tm, d), lambda i, j: (i, 0)),
                  pl.BlockSpec((1, d), lambda i, j: (0, 0)),
                  pl.BlockSpec((d, tn), lambda i, j: (0, j))],
        out_specs=pl.BlockSpec((None, tm, tn), lambda i, j: (j, i, 0)),
        out_shape=jax.ShapeDtypeStruct((n // tn, r, tn), F32),
        scratch_shapes=[pltpu.VMEM((tm, d), BF16)],
        compiler_params=_cparams(("arbitrary", "arbitrary")), name="in_proj")(x, g.reshape(1, d), w_bf16)


def _proj_spec(proj, rows, blk, row_col, **kw):
    tn = proj.shape[2]
    assert tn % blk == 0

    def index(*ids):
        rb, cb = row_col(*ids)
        return (cb * blk) // tn, rb, ((cb * blk) % tn) // blk

    return pl.BlockSpec((None, rows, blk), index, **kw)


def _sb_weights(z, mask, tail, tri2, latest_first):
    nk = tri2.shape[1]
    n = z.shape[1] // nk
    sp = _softplus(z)
    if mask is not None:
        sp = jnp.where(mask, sp, 0.0)
    hi = sp.astype(BF16)
    lo = (sp - hi.astype(F32)).astype(BF16)
    parts = [None] * n
    for c in (range(n) if latest_first else reversed(range(n))):
        cols = slice(c * nk, (c + 1) * nk)
        local = _dot(jnp.concatenate([hi[:, cols], lo[:, cols]], axis=1), tri2)
        parts[c] = local + tail
        tail = tail + local[:, 0:1]
    rsum = parts[0] if n == 1 else jnp.concatenate(parts, axis=1)
    w = jnp.exp(z - rsum)
    if mask is not None:
        w = jnp.where(mask, w, 0.0)
    return w.astype(BF16), tail


def _tri2(n):
    tri = (jnp.arange(n)[:, None] >= jnp.arange(n)[None, :]).astype(BF16)
    return jnp.concatenate([tri, tri], axis=0)


def _prompt_attn_step(hp, qi, half, bias_ref, q_ref, k_ref, v_ref, tri_ref, o_ref, kb_ref, vb_ref, acc_ref, znext_ref,
                      tail_ref, *, tq, tk, side_work=None):
    assert tq % tk == 0 or tk % tq == 0
    nd = max(tq // tk, 1)
    first_diag = (qi * tq) // tk

    lane = lax.broadcasted_iota(jnp.int32, (1, LANES), 1)
    first = lane < HEAD_DIM
    q = q_ref[...] * (HEAD_DIM ** -0.5)
    qh = (jnp.where(first, q, 0.0).astype(BF16), jnp.where(first, 0.0, q).astype(BF16))
    bias = (bias_ref[HEADS_PER_BLOCK * hp], bias_ref[HEADS_PER_BLOCK * hp + 1])
    tri2 = tri_ref[...]
    def logits(j, h):
        start = pl.multiple_of(j * tk, tk)
        return _dot_nt(qh[h], kb_ref[pl.ds(start, tk), :]) + bias[h]

    def weights(z, j, tail, masked):
        mask = None
        if masked:
            qpos = qi * tq + lax.broadcasted_iota(jnp.int32, (tq, tk), 0)
            mask = j * tk + lax.broadcasted_iota(jnp.int32, (tq, tk), 1) < qpos
        return _sb_weights(z, mask, tail, tri2, latest_first=False)

    def times_v(w, j):
        return _dot(w, vb_ref[pl.ds(pl.multiple_of(j * tk, tk), tk), :])

    assert HEADS_PER_BLOCK == 2

    def body(it, tails):
        j = first_diag - 1 - it
        z1 = znext_ref[...]
        z0 = logits(j, 0)
        w1, t1 = weights(z1, j, tails[1], False)
        acc_ref[1] += times_v(w1, j)
        w0, t0 = weights(z0, j, tails[0], False)
        acc_ref[0] += times_v(w0, j)
        znext_ref[...] = logits(jnp.maximum(j - 1, 0), 1)
        return (t0, t1)

    def below_diagonal(lo, hi, tails):
        return lax.fori_loop(lo, hi, body, tails)

    def start(n_off):
        @pl.when(qi == 0)
        def _():
            kb_ref[...] = k_ref[...].astype(BF16)
            vb_ref[...] = v_ref[...].astype(BF16)

        if side_work is not None:
            side_work()
        tails = [jnp.zeros((tq, 1), F32)] * HEADS_PER_BLOCK
        for dd in reversed(range(nd)):
            for h in range(HEADS_PER_BLOCK):
                j = first_diag + dd
                w, tails[h] = weights(logits(j, h), j, tails[h], True)
                if dd == nd - 1:
                    acc_ref[h] = times_v(w, j)
                else:
                    acc_ref[h] += times_v(w, j)
        znext_ref[...] = logits(jnp.maximum(first_diag - 1, 0), 1)
        return below_diagonal(0, n_off, tuple(tails))

    def finish(lo, tails):
        below_diagonal(lo, first_diag, tails)
        o_ref[...] = jnp.where(first, acc_ref[0], acc_ref[1])

    if half is None:
        finish(first_diag, start(first_diag))
    else:
        split = first_diag // 2

        @pl.when(half == 0)
        def _():
            tails = start(split)
            for h in range(HEADS_PER_BLOCK):
                tail_ref[h] = tails[h]

        @pl.when(half == 1)
        def _():
            if side_work is not None:
                side_work()
            finish(split, tuple(tail_ref[h] for h in range(HEADS_PER_BLOCK)))


def _attn_prompt_kernel(bias_ref, q_ref, k_ref, v_ref, tri_ref, o_ref, kb_ref, vb_ref, acc_ref, znext_ref, *, tq, tk):
    _prompt_attn_step(pl.program_id(1), pl.program_id(2), None, bias_ref, q_ref, k_ref, v_ref, tri_ref, o_ref,
                      kb_ref, vb_ref, acc_ref, znext_ref, None, tq=tq, tk=tk)


def _attn_prompt(proj, sb_bias, bsz, seq, width):
    r = proj.shape[1]
    tk = _pick_tile(seq, ATTN_K_TILE, LANES)
    tq = _pick_tile(seq, ATTN_Q_TILE, tk)
    tc = _pick_tile(tk, ATTN_CUMSUM_BLOCK, LANES)
    nq = seq // tq
    nhp = width // LANES
    return pl.pallas_call(
        functools.partial(_attn_prompt_kernel, tq=tq, tk=tk),
        grid=(bsz, nhp, nq),
        in_specs=[pl.BlockSpec(memory_space=pltpu.SMEM),
                  _proj_spec(proj, tq, LANES, lambda b, hp, qi: (b * nq + qi, hp)),
                  _proj_spec(proj, seq, LANES, lambda b, hp, qi: (b, nhp + hp)),
                  _proj_spec(proj, seq, LANES, lambda b, hp, qi: (b, 2 * nhp + hp)),
                  pl.BlockSpec((2 * tc, tc), lambda b, hp, qi: (0, 0))],
        out_specs=pl.BlockSpec((tq, LANES), lambda b, hp, qi: (b * nq + qi, hp)),
        out_shape=jax.ShapeDtypeStruct((r, width), F32),
        scratch_shapes=[pltpu.VMEM((seq, LANES), BF16), pltpu.VMEM((seq, LANES), BF16),
                        pltpu.VMEM((HEADS_PER_BLOCK, tq, LANES), F32), pltpu.VMEM((tq, tk), F32)],
        compiler_params=_cparams(("arbitrary", "arbitrary", "arbitrary")), name="attn_prompt")(
            sb_bias, proj, proj, proj, _tri2(tc))


def _sample_attn_parts(q_ref, kn_ref, vn_ref, bias_ref, tri_ref, k_refs, v_refs,
                       o_ref, wq_ref, acc_ref, tail_ref, pad_ref, *, n_heads, n_q):
    nrow = n_q * n_heads
    width = n_heads * HEAD_DIM
    assert n_heads & (n_heads - 1) == 0 and HEAD_DIM & (HEAD_DIM - 1) == 0
    head_shift, dim_shift = n_heads.bit_length() - 1, HEAD_DIM.bit_length() - 1
    r_head = lax.broadcasted_iota(jnp.int32, (nrow, width), 0) & (n_heads - 1)
    c_head = lax.broadcasted_iota(jnp.int32, (nrow, width), 1) >> dim_shift
    own_head = r_head == c_head
    page = tri_ref.shape[1]
    tri2 = tri_ref[...]
    bias = bias_ref[...]

    def first():
        q = q_ref[0] * (HEAD_DIM ** -0.5)
        rows = jnp.concatenate([jnp.broadcast_to(q[i:i + 1], (n_heads, width)) for i in range(n_q)], axis=0)
        wq_ref[...] = jnp.where(own_head, rows, 0.0).astype(BF16)
        pad_ref[...] = jnp.zeros_like(pad_ref)
        pad_ref[0:SUBLANES, :] = kn_ref[0]
        z = _dot_nt(wq_ref[...], pad_ref[...].astype(BF16)) + bias
        pad_ref[0:SUBLANES, :] = vn_ref[0]
        kpos = lax.broadcasted_iota(jnp.int32, (nrow, page), 1)
        qpos = lax.broadcasted_iota(jnp.int32, (nrow, page), 0) >> head_shift
        w, tl = _sb_weights(z, kpos < qpos, jnp.zeros((nrow, 1), F32), tri2, latest_first=True)
        acc_ref[...] = _dot(w, pad_ref[...].astype(BF16))
        tail_ref[...] = tl

    def pages():
        tail = tail_ref[...]
        acc = acc_ref[...]
        for g0 in range(0, len(k_refs), SAMPLE_PAGE_GROUP):
            kcat = jnp.concatenate([r[0].astype(BF16) for r in k_refs[g0:g0 + SAMPLE_PAGE_GROUP]], axis=1)
            vcat = jnp.concatenate([r[0].astype(BF16) for r in v_refs[g0:g0 + SAMPLE_PAGE_GROUP]], axis=1)
            z = _dot(wq_ref[...], kcat) + bias
            w, tail = _sb_weights(z, None, tail, tri2, latest_first=True)
            acc = acc + _dot_nt(w, vcat)
        acc_ref[...] = acc
        tail_ref[...] = tail

    def last():
        own = jnp.where(own_head, acc_ref[...], 0.0)
        o_ref[0] = jnp.concatenate(
            [jnp.sum(own[i * n_heads:(i + 1) * n_heads], axis=0, keepdims=True) for i in range(n_q)], axis=0)

    return first, pages, last


def _attn_sample_kernel(pt_ref, q_ref, kn_ref, vn_ref, bias_ref, tri_ref, *rest, n_heads, n_q, pages_per_step):
    del pt_ref
    k_refs = rest[:pages_per_step]
    v_refs = rest[pages_per_step:2 * pages_per_step]
    s = pl.program_id(1)
    first, pages, last = _sample_attn_parts(q_ref, kn_ref, vn_ref, bias_ref, tri_ref, k_refs, v_refs,
                                            *rest[2 * pages_per_step:], n_heads=n_heads, n_q=n_q)
    pl.when(s == 0)(first)
    pages()
    pl.when(s == pl.num_programs(1) - 1)(last)


def _sample_attn_operands(q, k_new, v_new, sb_bias, cache_k, cache_v, bs, n_q, pps, seq_and_group):
    n_pool, page, n_heads, hd = cache_k.shape
    width = n_heads * hd
    nrow = n_q * n_heads
    ck = cache_k.transpose(0, 2, 3, 1).reshape(n_pool, width, page)
    cv = cache_v.transpose(0, 2, 3, 1).reshape(n_pool, width, page)
    pad = lambda a: jnp.pad(a.reshape(bs, n_q, width), ((0, 0), (0, SUBLANES - n_q), (0, 0)))
    bias_col = jnp.tile(sb_bias, n_q).reshape(nrow, 1)

    def per_seq(rows):
        return pl.BlockSpec((1, rows, width), lambda *ids: (seq_and_group(*ids[:-1])[0], 0, 0))

    def page_spec(p):
        def index(*ids):
            pt = ids[-1]
            sb, grp = seq_and_group(*ids[:-1])
            return pt[sb, pt.shape[1] - 1 - (grp * pps + p)], 0, 0
        return pl.BlockSpec((1, width, page), index)

    const = lambda shape: pl.BlockSpec(shape, lambda *ids: (0, 0))
    args = [q.reshape(bs, n_q, width), pad(k_new), pad(v_new), bias_col, _tri2(page)] + [ck] * pps + [cv] * pps
    in_specs = ([per_seq(n_q), per_seq(SUBLANES), per_seq(SUBLANES), const((nrow, 1)), const((2 * page, page))]
                + [page_spec(p) for p in range(pps)] * 2)
    scratch = [pltpu.VMEM((nrow, width), BF16), pltpu.VMEM((nrow, width), F32),
               pltpu.VMEM((nrow, 1), F32), pltpu.VMEM((page, width), F32)]
    return args, in_specs, per_seq(n_q), jax.ShapeDtypeStruct((bs, n_q, width), F32), scratch


def _attn_sample(q, k_new, v_new, sb_bias, cache_k, cache_v, page_table, bs, n_q):
    n_heads = cache_k.shape[2]
    n_pages = page_table.shape[1]
    pps = _pick_tile(n_pages, SAMPLE_PAGE_GROUP, 1)
    args, in_specs, out_spec, out_shape, scratch = _sample_attn_operands(
        q, k_new, v_new, sb_bias, cache_k, cache_v, bs, n_q, pps, lambda b, s: (b, s))
    grid_spec = pltpu.PrefetchScalarGridSpec(
        num_scalar_prefetch=1, grid=(bs, n_pages // pps), in_specs=in_specs, out_specs=out_spec,
        scratch_shapes=scratch)
    out = pl.pallas_call(
        functools.partial(_attn_sample_kernel, n_heads=n_heads, n_q=n_q, pages_per_step=pps),
        grid_spec=grid_spec, out_shape=out_shape,
        compiler_params=_cparams(("arbitrary", "arbitrary")), name="attn_sample")(page_table, *args)
    return out.reshape(bs * n_q, -1)


def _attn_fused_kernel(pt_ref, bias_ref, q_ref, k_ref, v_ref, tri_ref, *rest, tq, tk, n_heads, n_q, pages_per_step,
                       steps_per_seq):
    del pt_ref
    sq_ref, kn_ref, vn_ref, sbias_ref, stri_ref = rest[:5]
    k_refs = rest[5:5 + pages_per_step]
    v_refs = rest[5 + pages_per_step:5 + 2 * pages_per_step]
    (o_ref, os_ref, kb_ref, vb_ref, acc_ref, znext_ref, ptail_ref,
     wq_ref, sacc_ref, tail_ref, pad_ref) = rest[5 + 2 * pages_per_step:]
    hp, qi, half = pl.program_id(1), pl.program_id(2), pl.program_id(3)
    lin = ((pl.program_id(0) * pl.num_programs(1) + hp) * pl.num_programs(2) + qi) * pl.num_programs(3) + half
    grp = lin % steps_per_seq
    first, pages, last = _sample_attn_parts(sq_ref, kn_ref, vn_ref, sbias_ref, stri_ref, k_refs, v_refs,
                                            os_ref, wq_ref, sacc_ref, tail_ref, pad_ref, n_heads=n_heads, n_q=n_q)
    pl.when(grp == 0)(first)
    _prompt_attn_step(hp, qi, half, bias_ref, q_ref, k_ref, v_ref, tri_ref, o_ref, kb_ref, vb_ref, acc_ref, znext_ref,
                      ptail_ref, tq=tq, tk=tk, side_work=pages)
    pl.when(grp == steps_per_seq - 1)(last)


def _attn_fused(proj, sb_bias, bsz, seq, width, q_s, k_new, v_new, cache_k, cache_v, page_table, bs, n_q):
    r = proj.shape[1]
    n_heads = cache_k.shape[2]
    n_pages = page_table.shape[1]
    tk = _pick_tile(seq, ATTN_K_TILE, LANES)
    tq = _pick_tile(seq, ATTN_Q_TILE, tk)
    tc = _pick_tile(tk, ATTN_CUMSUM_BLOCK, LANES)
    nq = seq // tq
    nhp = width // LANES
    halves = 2
    n_steps = bsz * nhp * nq * halves
    pps = (bs * n_pages) // n_steps
    if pps == 0 or pps * n_steps != bs * n_pages or n_pages % pps or pps > 2 * SAMPLE_PAGE_GROUP:
        return None
    sps = n_pages // pps

    def seq_and_group(b, hp, qi, half):
        lin = ((b * nhp + hp) * nq + qi) * halves + half
        return lin // sps, lin % sps

    s_args, s_specs, s_out_spec, s_out_shape, s_scratch = _sample_attn_operands(
        q_s, k_new, v_new, sb_bias, cache_k, cache_v, bs, n_q, pps, seq_and_group)
    whole_seq = lambda c0: _proj_spec(proj, seq, LANES, lambda b, hp, qi, half, pt: (b, c0 + hp),
                                      pipeline_mode=pl.Buffered(1))
    q_index = lambda b, hp, qi, half, pt: (b * nq + qi, hp)
    grid_spec = pltpu.PrefetchScalarGridSpec(
        num_scalar_prefetch=1, grid=(bsz, nhp, nq, halves),
        in_specs=[pl.BlockSpec(memory_space=pltpu.SMEM), _proj_spec(proj, tq, LANES, q_index),
                  whole_seq(nhp), whole_seq(2 * nhp),
                  pl.BlockSpec((2 * tc, tc), lambda b, hp, qi, half, pt: (0, 0))] + s_specs,
        out_specs=[pl.BlockSpec((tq, LANES), q_index), s_out_spec],
        scratch_shapes=[pltpu.VMEM((seq, LANES), BF16), pltpu.VMEM((seq, LANES), BF16),
                        pltpu.VMEM((HEADS_PER_BLOCK, tq, LANES), F32), pltpu.VMEM((tq, tk), F32),
                        pltpu.VMEM((HEADS_PER_BLOCK, tq, 1), F32)] + s_scratch)
    o_p, o_s = pl.pallas_call(
        functools.partial(_attn_fused_kernel, tq=tq, tk=tk, n_heads=n_heads, n_q=n_q, pages_per_step=pps,
                          steps_per_seq=sps),
        grid_spec=grid_spec, out_shape=[jax.ShapeDtypeStruct((r, width), F32), s_out_shape],
        compiler_params=_cparams(("arbitrary",) * 4), name="attn_fused")(
            page_table, sb_bias, proj, proj, proj, _tri2(tc), *s_args)
    return o_p, o_s.reshape(bs * n_q, -1)


def _ssm_kernel(*refs, chunks_per_seq, per_row_h0):
    if per_row_h0:
        u_ref, bw_ref, cw_ref, k_ref, d_ref, h0_ref, gg_ref, hs_ref, bu_ref = refs
    else:
        u_ref, bw_ref, cw_ref, k_ref, d_ref, gg_ref, hs_ref, bu_ref, carry_ref = refs
    i, j = pl.program_id(0), pl.program_id(1)
    rows = u_ref.shape[0]
    half = bu_ref.shape[1] // 2
    u = u_ref[...]
    bu_ref[...] = _dot(u.astype(BF16), bw_ref[0])

    if not per_row_h0:
        @pl.when(i % chunks_per_seq == 0)
        def _():
            carry_ref[j] = jnp.zeros((1, 2 * half), F32)

    def tile(tix, carry):
        r0 = pl.multiple_of(tix * SUBLANES, SUBLANES)
        br = bu_ref[pl.ds(r0, SUBLANES), 0:half]
        bi = bu_ref[pl.ds(r0, SUBLANES), half:2 * half]
        for n, d in enumerate((1, 2, 4)):
            ar, ai = k_ref[0, 2 * n], k_ref[0, 2 * n + 1]
            sr, si = pltpu.roll(br, d, 0), pltpu.roll(bi, d, 0)
            br, bi = br + ar * sr - ai * si, bi + ar * si + ai * sr
        pr, pi = k_ref[0, 6], k_ref[0, 7]
        if per_row_h0:
            cr = h0_ref[pl.ds(r0, SUBLANES), 0:half]
            ci = h0_ref[pl.ds(r0, SUBLANES), half:2 * half]
        else:
            cr, ci = carry
        hr = br + pr * cr - pi * ci
        hi = bi + pr * ci + pi * cr
        bu_ref[pl.ds(r0, SUBLANES), 0:half] = hr
        bu_ref[pl.ds(r0, SUBLANES), half:2 * half] = hi
        return (hr[SUBLANES - 1:SUBLANES], hi[SUBLANES - 1:SUBLANES])

    if per_row_h0:
        init = (jnp.zeros((1, half), F32),) * 2
    else:
        init = (carry_ref[j, :, 0:half], carry_ref[j, :, half:2 * half])
    last = lax.fori_loop(0, rows // SUBLANES, tile, init, unroll=4)
    if per_row_h0:
        hs_ref[...] = bu_ref[...]
    else:
        carry_ref[j, :, 0:half] = last[0]
        carry_ref[j, :, half:2 * half] = last[1]
        hs_ref[0, 0] = bu_ref[rows - SUBLANES:rows, :]
    y = _dot(bu_ref[...].astype(BF16), cw_ref[0]) + d_ref[...] * u
    gg_ref[...] = _gelu(y)


def _ssm(proj, u_col, bblk, cblk, consts, d_skip, seq, tc, h0_rows=None):
    r = proj.shape[1]
    width = d_skip.shape[0]
    nb, cb, sw = bblk.shape
    per_row = h0_rows is not None
    cps = max(seq // tc, 1)
    u_blk = u_col // cb
    assert u_blk * cb == u_col
    in_specs = [_proj_spec(proj, tc, cb, lambda i, j: (i, u_blk + j)),
                pl.BlockSpec((1, cb, sw), lambda i, j: (j, 0, 0)),
                pl.BlockSpec((1, sw, cb), lambda i, j: (j, 0, 0)),
                pl.BlockSpec((1, 8, SUBLANES, sw // 2), lambda i, j: (j, 0, 0, 0)),
                pl.BlockSpec((1, cb), lambda i, j: (0, j))]
    args = [proj, bblk, cblk, consts, d_skip.reshape(1, width)]
    scratch = [pltpu.VMEM((tc, sw), F32)]
    if per_row:
        assert r == tc
        in_specs.append(pl.BlockSpec((tc, sw), lambda i, j: (0, j)))
        args.append(h0_rows)
        hs_shape = jax.ShapeDtypeStruct((r, nb * sw), F32)
        hs_spec = pl.BlockSpec((tc, sw), lambda i, j: (0, j))
    else:
        hs_shape = jax.ShapeDtypeStruct((r // tc, nb, SUBLANES, sw), F32)
        hs_spec = pl.BlockSpec((1, 1, SUBLANES, sw), lambda i, j: (i, j, 0, 0))
        scratch.append(pltpu.VMEM((nb, 1, sw), F32))
    return pl.pallas_call(
        functools.partial(_ssm_kernel, chunks_per_seq=cps, per_row_h0=per_row),
        grid=(r // tc, nb), in_specs=in_specs,
        out_specs=[pl.BlockSpec((tc, cb), lambda i, j: (i, j)), hs_spec],
        out_shape=[jax.ShapeDtypeStruct((r, width), F32), hs_shape],
        scratch_shapes=scratch,
        compiler_params=_cparams(("arbitrary", "arbitrary")), name="ssm")(*args)


def _mix_kernel(oa_ref, gg_ref, x_ref, wglu_ref, bglu_ref, wba_ref, wbs_ref, wout_ref, g_ref, *rest):
    gate_refs, x1_ref = rest[:-1], rest[-1]
    half = len(gate_refs) // 2
    ga = jnp.concatenate([ref[...] for ref in gate_refs[:half]], axis=1)
    gs = jnp.concatenate([ref[...] for ref in gate_refs[half:]], axis=1)
    gg = gg_ref[...]
    o_ssm = gg * _sigmoid(_dot(gg.astype(BF16), wglu_ref[...]) + bglu_ref[...])
    merged = (_sigmoid(ga) * _dot(oa_ref[...].astype(BF16), wba_ref[...])
              + _sigmoid(gs) * _dot(o_ssm.astype(BF16), wbs_ref[...]))
    mo = _dot(merged.astype(BF16), wout_ref[...])
    x1_ref[...] = x_ref[...] + _rms(mo) * g_ref[...]


def _mix(o_attn, gg, proj, gate_col, x, wglu, bglu, wba, wbs, wout, g, tm):
    r, d = x.shape
    w = o_attn.shape[1]
    tn = proj.shape[2]
    gblk = gate_col // tn
    assert gblk * tn == gate_col and d % tn == 0
    n_gate = 2 * (d // tn)
    row = lambda n: pl.BlockSpec((tm, n), lambda i: (i, 0))
    const = lambda a, b: pl.BlockSpec((a, b), lambda i: (0, 0), pipeline_mode=pl.Buffered(1))
    gate = lambda c: _proj_spec(proj, tm, tn, lambda i: (i, gblk + c))
    return pl.pallas_call(
        _mix_kernel, grid=(r // tm,),
        in_specs=[row(w), row(w), row(d), const(w, w), const(1, w), const(w, d), const(w, d), const(d, d),
                  const(1, d)] + [gate(c) for c in range(n_gate)],
        out_specs=row(d), out_shape=jax.ShapeDtypeStruct((r, d), F32),
        compiler_params=_cparams(("arbitrary",)), name="mix")(
            o_attn, gg, x, wglu, bglu.reshape(1, w), wba, wbs, wout, g.reshape(1, d), *([proj] * n_gate))


def _ffn_kernel(*refs, seq, has_prev, conv_w):
    if has_prev:
        (x_ref, g3_ref, wg_ref, wv_ref, cw_ref, cb_ref, wd_ref, g4_ref, p1_ref, p2_ref,
         y_ref, cs_ref, xn_ref, acc_ref, gbuf_ref, carry_ref) = refs
    else:
        (x_ref, g3_ref, wg_ref, wv_ref, cw_ref, cb_ref, wd_ref, g4_ref,
         y_ref, cs_ref, xn_ref, acc_ref, gbuf_ref, carry_ref) = refs
    i, j = pl.program_id(0), pl.program_id(1)
    tm = x_ref.shape[0]
    assert conv_w == 3

    @pl.when(j == 0)
    def _():
        xn_ref[...] = (_rms(x_ref[...]) * g3_ref[...]).astype(BF16)
        acc_ref[...] = jnp.zeros_like(acc_ref)

    @pl.when(i == 0)
    def _():
        carry_ref[j] = jnp.zeros(carry_ref.shape[1:], F32)

    xn = xn_ref[...]
    t = (i * tm + lax.broadcasted_iota(jnp.int32, (tm, 1), 0)) % seq
    keep = cs_ref.shape[1]
    tn = wg_ref.shape[1]
    part = min(tn, FFN_COL_PART)
    for c0 in range(0, tn, part):
        cols = slice(c0, c0 + part)
        gate = _dot(xn, wg_ref[:, cols])
        val = _dot(xn, wv_ref[:, cols])
        gbuf_ref[0:SUBLANES, cols] = carry_ref[j, :, cols]
        gbuf_ref[SUBLANES:SUBLANES + tm, cols] = gate
        g1 = gbuf_ref[pl.ds(SUBLANES - 1, tm), cols]
        g2 = gbuf_ref[pl.ds(SUBLANES - 2, tm), cols]
        g1 = jnp.where(t >= 1, g1, p1_ref[:, cols] if has_prev else 0.0)
        g2 = jnp.where(t >= 2, g2, p2_ref[:, cols] if has_prev else 0.0)
        cw = cw_ref[:, cols]
        conv = cb_ref[:, cols] + cw[0:1] * g2 + cw[1:2] * g1 + cw[2:3] * gate
        h = _gelu(conv) * val
        acc_ref[...] += _dot(h.astype(BF16), wd_ref[cols, :])
        carry_ref[j, :, cols] = gate[tm - SUBLANES:tm]
        cs_ref[0, :, cols] = gate[tm - keep:tm]

    @pl.when(j == pl.num_programs(1) - 1)
    def _():
        y_ref[...] = x_ref[...] + _rms(acc_ref[...]) * g4_ref[...]


def _ffn(x1, g3, wg, wv, cw, cb, wd, g4, seq, tm, prev=None):
    r, d = x1.shape
    dffp = wg.shape[1]
    tn = FFN_COL_TILE
    nj = dffp // tn
    has_prev = prev is not None
    keep = tm if has_prev else SUBLANES
    in_specs = [pl.BlockSpec((tm, d), lambda i, j: (i, 0)),
                pl.BlockSpec((1, d), lambda i, j: (0, 0)),
                pl.BlockSpec((d, tn), lambda i, j: (0, j)),
                pl.BlockSpec((d, tn), lambda i, j: (0, j)),
                pl.BlockSpec((3, tn), lambda i, j: (0, j)),
                pl.BlockSpec((1, tn), lambda i, j: (0, j)),
                pl.BlockSpec((tn, d), lambda i, j: (j, 0)),
                pl.BlockSpec((1, d), lambda i, j: (0, 0))]
    args = [x1, g3.reshape(1, d), wg, wv, cw, cb.reshape(1, dffp), wd, g4.reshape(1, d)]
    if has_prev:
        assert r == tm
        in_specs += [pl.BlockSpec((tm, tn), lambda i, j: (0, j))] * 2
        args += list(prev)
    n_cs = r // tm
    return pl.pallas_call(
        functools.partial(_ffn_kernel, seq=seq, has_prev=has_prev, conv_w=cw.shape[0]),
        grid=(r // tm, nj), in_specs=in_specs,
        out_specs=[pl.BlockSpec((tm, d), lambda i, j: (i, 0)),
                   pl.BlockSpec((1, keep, tn), lambda i, j: (i, 0, j))],
        out_shape=[jax.ShapeDtypeStruct((r, d), F32), jax.ShapeDtypeStruct((n_cs, keep, dffp), F32)],
        scratch_shapes=[pltpu.VMEM((tm, d), BF16), pltpu.VMEM((tm, d), F32),
                        pltpu.VMEM((tm + SUBLANES, tn), F32), pltpu.VMEM((nj, SUBLANES, tn), F32)],
        compiler_params=_cparams(("arbitrary", "arbitrary")), name="ffn")(*args)


def _layer(xp, xs, cache_k, cache_v, page_table, h0_re, h0_im, conv_prev, wts):
    (norm_mix_pre, norm_mix_post, norm_ffn_pre, norm_ffn_post, w_in, sb_bias, a_re, a_im, log_dt, b_re, b_im,
     c_re, c_im, d_skip, w_glu, b_glu, w_ba, w_bs, w_out, w_up, conv_w, conv_b, w_down) = wts
    bp, seq, d = xp.shape
    bs, nq, _ = xs.shape
    n_heads = sb_bias.shape[0]
    aw = n_heads * HEAD_DIM
    g, p = a_re.shape
    c = b_re.shape[-1]
    sw_total = g * c
    gb = SSM_BLOCK_CH // c
    nb = g // gb
    dff = conv_w.shape[1]
    dffp = -(-dff // FFN_COL_TILE) * FFN_COL_TILE
    u_col, gate_col = 3 * aw, 3 * aw + sw_total
    assert w_in.shape[1] == gate_col + 2 * d

    w_in_b, w_glu_b, w_ba_b, w_bs_b, w_out_b = (a.astype(BF16) for a in (w_in, w_glu, w_ba, w_bs, w_out))
    padc = lambda a: jnp.pad(a, ((0, 0), (0, dffp - dff)))
    wg_b = padc(w_up[:, :dff]).astype(BF16)
    wv_b = padc(w_up[:, dff:]).astype(BF16)
    wd_b = jnp.pad(w_down, ((0, dffp - dff), (0, 0))).astype(BF16)
    cw_p, cb_p = padc(conv_w), jnp.pad(conv_b, (0, dffp - dff))

    powr, powi, bbr, bbi = _ssm_prep(a_re, a_im, log_dt, b_re, b_im)
    bblk, cblk = _ssm_block_weights(bbr, bbi, c_re, c_im)

    def states(h):
        h = h.reshape(-1, nb, 2, gb, p)
        return h[:, :, 0].reshape(-1, g, p), h[:, :, 1].reshape(-1, g, p)

    rp = bp * seq
    x2 = xp.reshape(rp, d)
    rs = bs * nq
    assert SUBLANES % nq == 0
    xs2 = xs.reshape(rs, d)
    proj = _in_proj(x2, norm_mix_pre, w_in_b, _pick_tile(rp, IN_PROJ_ROW_TILE))
    proj_s = _in_proj(xs2, norm_mix_pre, w_in_b, rs)
    assert proj.shape[2] == aw
    k, v = proj[1], proj[2]
    qs, ks, vs = proj_s[0], proj_s[1], proj_s[2]
    fused = _attn_fused(proj, sb_bias, bp, seq, aw, qs, ks, vs, cache_k, cache_v, page_table, bs, nq)
    if fused is None:
        fused = (_attn_prompt(proj, sb_bias, bp, seq, aw),
                 _attn_sample(qs, ks, vs, sb_bias, cache_k, cache_v, page_table, bs, nq))
    o_attn, o_attn_s = fused
    tc = _pick_tile(seq, SSM_ROW_CHUNK)
    gg, hs = _ssm(proj, u_col, bblk, cblk, _scan_consts(powr, powi, SUBLANES, gb), d_skip, seq, tc)
    hrp, hip = states(hs.reshape(bp, seq // tc, nb, SUBLANES, -1)[:, -1, :, SUBLANES - 1, :].reshape(bp, -1))
    x1 = _mix(o_attn, gg, proj, gate_col, x2, w_glu_b, b_glu, w_ba_b, w_bs_b, w_out_b, norm_mix_post,
              _pick_tile(rp, 256))
    tmf = _pick_tile(seq, 512)
    yp, csp = _ffn(x1, norm_ffn_pre, wg_b, wv_b, cw_p, cb_p, wd_b, norm_ffn_post, seq, tmf)
    out_p = (yp.reshape(bp, seq, d), k.reshape(bp, seq, n_heads, HEAD_DIM), v.reshape(bp, seq, n_heads, HEAD_DIM),
             hrp, hip, csp.reshape(bp, seq // tmf, SUBLANES, dffp)[:, -1, SUBLANES - (conv_w.shape[0] - 1):, :dff])

    proj, o_attn, k, v = proj_s, o_attn_s, ks, vs
    h0 = jnp.stack([h0_re.reshape(bs, nb, gb * p), h0_im.reshape(bs, nb, gb * p)], axis=2).reshape(bs, nb * 2 * gb * p)
    gg, hs = _ssm(proj, u_col, bblk, cblk, _scan_consts(powr, powi, nq, gb), d_skip, nq, rs,
                  h0_rows=jnp.repeat(h0, nq, axis=0))
    hrs, his = states(hs.reshape(bs, nq, -1)[:, nq - 1])
    x1 = _mix(o_attn, gg, proj, gate_col, xs2, w_glu_b, b_glu, w_ba_b, w_bs_b, w_out_b, norm_mix_post, rs)
    t_row = jnp.arange(rs) % nq
    prev = padc(conv_prev.reshape(bs * (conv_w.shape[0] - 1), dff)).reshape(bs, conv_w.shape[0] - 1, dffp)
    prev_rows = jnp.repeat(prev, nq, axis=0)
    p1 = jnp.where((t_row == 0)[:, None], prev_rows[:, 1], 0.0)
    p2 = jnp.where((t_row == 0)[:, None], prev_rows[:, 0], jnp.where((t_row == 1)[:, None], prev_rows[:, 1], 0.0))
    ys, css = _ffn(x1, norm_ffn_pre, wg_b, wv_b, cw_p, cb_p, wd_b, norm_ffn_post, nq, rs, prev=(p1, p2))
    css = css.reshape(bs, nq, dffp)[:, nq - (conv_w.shape[0] - 1):, :dff]
    out_s = (ys.reshape(bs, nq, d), k.reshape(bs, nq, n_heads, HEAD_DIM), v.reshape(bs, nq, n_heads, HEAD_DIM),
             hrs, his, css)
    return out_p, out_s


def kernel(x_prompt, x_sample, cache_k, cache_v, page_table, state_ssm_re, state_ssm_im, state_ffn_conv, norm_mix_pre, norm_mix_post, norm_ffn_pre, norm_ffn_post, w_in, sb_bias, ssm_a_re, ssm_a_im, ssm_log_dt, ssm_b_re, ssm_b_im, ssm_c_re, ssm_c_im, ssm_d, ssm_w_glu, ssm_b_glu, w_branch_attn, w_branch_ssm, w_out, w_up, conv_w, conv_b, w_down):
    depth = w_in.shape[0]
    xp, xs = x_prompt, x_sample
    acc_p, acc_s = [], []
    for layer in range(depth):
        wts = tuple(a[layer] for a in (
            norm_mix_pre, norm_mix_post, norm_ffn_pre, norm_ffn_post, w_in, sb_bias, ssm_a_re, ssm_a_im, ssm_log_dt,
            ssm_b_re, ssm_b_im, ssm_c_re, ssm_c_im, ssm_d, ssm_w_glu, ssm_b_glu, w_branch_attn, w_branch_ssm, w_out,
            w_up, conv_w, conv_b, w_down))
        out_p, out_s = _layer(xp, xs, cache_k[layer], cache_v[layer], page_table, state_ssm_re[layer],
                              state_ssm_im[layer], state_ffn_conv[layer], wts)
        xp, xs = out_p[0], out_s[0]
        acc_p.append(out_p[1:])
        acc_s.append(out_s[1:])
    stack = lambda acc, n: jnp.stack([a[n] for a in acc])
    kp, vp, hrp, hip, cp = (stack(acc_p, n) for n in range(5))
    ks, vs, hrs, his, cs = (stack(acc_s, n) for n in range(5))
    return (xp, xs, kp, vp, ks, vs, hrp, hip, hrs, his, cp, cs)
```
